```python
import jax
import jax.numpy as jnp
from jax import lax
import numpy as np

D_MODEL = 1024
BATCH = 8
SEQ = 2048
DEPTH = 4

N_A_LAYERS = DEPTH // 2
N_B_LAYERS = DEPTH - N_A_LAYERS
A_HEADS = 8
A_HEAD_DIM = D_MODEL // A_HEADS
MOBA_BLOCK = 256
MOBA_TOPK = 3
MOBA_Q_CHUNK = 64
B_HEADS = 8
QK_NOPE_DIM = 128
QK_ROPE_DIM = 64
V_HEAD_DIM = D_MODEL // B_HEADS
Q_LORA_RANK = D_MODEL // 2
KV_LORA_RANK = D_MODEL // 4
MLA_Q_BLOCK = 128
D_FF = ((8 * D_MODEL // 3 + 255) // 256) * 256
ROPE_THETA = 10000.0
NORM_EPS = 1e-6
N_SUBLAYER_NORMS = 6

kernel_name = "yoco_moba_mla_macaron_sandwich"


def rms_norm(x, g):
    xf = x.astype(jnp.float32)
    y = xf * lax.rsqrt(jnp.mean(xf * xf, axis=-1, keepdims=True) + NORM_EPS)
    return (y * g.astype(jnp.float32)).astype(x.dtype)


def rope_tables(seq_len, dim):
    inv_freq = 1.0 / (ROPE_THETA ** (jnp.arange(0, dim, 2, dtype=jnp.float32) / dim))
    ang = jnp.arange(seq_len, dtype=jnp.float32)[:, None] * inv_freq[None, :]
    return jnp.cos(ang), jnp.sin(ang)


def apply_rope(x, cos, sin):
    half = x.shape[-1] // 2
    xf = x.astype(jnp.float32)
    x1, x2 = xf[..., :half], xf[..., half:]
    return jnp.concatenate([x1 * cos - x2 * sin, x1 * sin + x2 * cos], axis=-1).astype(x.dtype)


def swiglu(h, w1, w3, w2):
    return (jax.nn.silu(h @ w1) * (h @ w3)) @ w2


def moba_attention(q, k, v):
    bsz, nh, s_len, dh = q.shape
    nb = -(-s_len // MOBA_BLOCK)
    pad = nb * MOBA_BLOCK - s_len
    k_blk = jnp.pad(k, ((0, 0), (0, 0), (0, pad), (0, 0))).reshape(bsz, nh, nb, MOBA_BLOCK, dh)
    v_blk = jnp.pad(v, ((0, 0), (0, 0), (0, pad), (0, 0))).reshape(bsz, nh, nb, MOBA_BLOCK, dh)
    k_mean = jnp.mean(k_blk.astype(jnp.float32), axis=3)
    k_sel = min(MOBA_TOPK, nb - 1)
    scale = dh ** -0.5
    key_off = jnp.arange(MOBA_BLOCK)
    n_chunks = s_len // MOBA_Q_CHUNK
    gather = jax.vmap(jax.vmap(lambda blocks, idx: blocks[idx]))

    def chunk(c):
        start = c * MOBA_Q_CHUNK
        qc = lax.dynamic_slice_in_dim(q, start, MOBA_Q_CHUNK, axis=2)
        qpos = start + jnp.arange(MOBA_Q_CHUNK)
        own = start // MOBA_BLOCK
        logits = []
        top_i = None
        if k_sel > 0:
            gate = jnp.einsum('bhqd,bhnd->bhqn', qc.astype(jnp.float32), k_mean)
            gate = jnp.where(jnp.arange(nb) < own, gate, -jnp.inf)
            _, top_i = lax.top_k(gate, k_sel)
            valid = jnp.arange(k_sel) < own
            for s in range(k_sel):
                kg = gather(k_blk, top_i[..., s])
                l = jnp.einsum('bhqd,bhqkd->bhqk', qc, kg).astype(jnp.float32) * scale
                logits.append(jnp.where(valid[s], l, -jnp.inf))
        k_own = lax.dynamic_index_in_dim(k_blk, own, axis=2, keepdims=False)
        v_own = lax.dynamic_index_in_dim(v_blk, own, axis=2, keepdims=False)
        l_own = jnp.einsum('bhqd,bhkd->bhqk', qc, k_own).astype(jnp.float32) * scale
        causal = (own * MOBA_BLOCK + key_off)[None, :] <= qpos[:, None]
        logits.append(jnp.where(causal, l_own, -jnp.inf))
        p = jax.nn.softmax(jnp.concatenate(logits, axis=-1), axis=-1).astype(v.dtype)
        out = jnp.einsum('bhqk,bhkd->bhqd', p[..., k_sel * MOBA_BLOCK:], v_own)
        for s in range(k_sel):
            vg = gather(v_blk, top_i[..., s])
            out = out + jnp.einsum('bhqk,bhqkd->bhqd', p[..., s * MOBA_BLOCK:(s + 1) * MOBA_BLOCK], vg)
        return out

    outs = lax.map(chunk, jnp.arange(n_chunks))
    return outs.transpose(1, 2, 0, 3, 4).reshape(bsz, nh, s_len, dh)


def moba_mixer(h, w_qkv, w_o, cos_h, sin_h):
    bsz, s_len, _ = h.shape
    qkv = (h @ w_qkv).reshape(bsz, s_len, 3, A_HEADS, A_HEAD_DIM)
    q = apply_rope(qkv[:, :, 0], cos_h, sin_h).transpose(0, 2, 1, 3)
    k = apply_rope(qkv[:, :, 1], cos_h, sin_h).transpose(0, 2, 1, 3)
    v = qkv[:, :, 2].transpose(0, 2, 1, 3)
    o = moba_attention(q, k, v)
    return o.transpose(0, 2, 1, 3).reshape(bsz, s_len, A_HEADS * A_HEAD_DIM) @ w_o


def mla_shared_kv(h, kv_in_norm, w_dkv_kr, kv_norm, w_ukv, cos_r, sin_r):
    bsz, s_len, _ = h.shape
    ckr = rms_norm(h, kv_in_norm) @ w_dkv_kr
    c_kv = rms_norm(ckr[..., :KV_LORA_RANK], kv_norm)
    k_rope = apply_rope(ckr[..., KV_LORA_RANK:], cos_r, sin_r)
    kv = (c_kv @ w_ukv).reshape(bsz, s_len, B_HEADS, QK_NOPE_DIM + V_HEAD_DIM)
    return kv[..., :QK_NOPE_DIM], k_rope, kv[..., QK_NOPE_DIM:]


def mla_mixer(h, w_dq, q_norm, w_uq, w_o, k_nope, k_rope, v, cos_rh, sin_rh):
    bsz, s_len, _ = h.shape
    cq = rms_norm(h @ w_dq, q_norm)
    q = (cq @ w_uq).reshape(bsz, s_len, B_HEADS, QK_NOPE_DIM + QK_ROPE_DIM)
    q_nope = q[..., :QK_NOPE_DIM]
    q_rope = apply_rope(q[..., QK_NOPE_DIM:], cos_rh, sin_rh)
    scale = (QK_NOPE_DIM + QK_ROPE_DIM) ** -0.5
    kpos = jnp.arange(s_len)

    def block(i):
        s0 = i * MLA_Q_BLOCK
        qn = lax.dynamic_slice_in_dim(q_nope, s0, MLA_Q_BLOCK, axis=1)
        qr = lax.dynamic_slice_in_dim(q_rope, s0, MLA_Q_BLOCK, axis=1)
        logits = (jnp.einsum('bqhd,bkhd->bhqk', qn, k_nope)
                  + jnp.einsum('bqhd,bkd->bhqk', qr, k_rope)).astype(jnp.float32) * scale
        qpos = s0 + jnp.arange(MLA_Q_BLOCK)
        logits = jnp.where(kpos[None, :] <= qpos[:, None], logits, -jnp.inf)
        p = jax.nn.softmax(logits, axis=-1).astype(v.dtype)
        return jnp.einsum('bhqk,bkhd->bqhd', p, v)

    o = lax.map(block, jnp.arange(s_len // MLA_Q_BLOCK))
    o = o.transpose(1, 0, 2, 3, 4).reshape(bsz, s_len, B_HEADS * V_HEAD_DIM)
    return o @ w_o


def setup_inputs(seed: int = 0) -> dict:
    key = jax.random.key(seed)
    ks = jax.random.split(key, 20)

    def w(k, shape, fan_in):
        return jax.random.normal(k, shape, jnp.float32) * fan_in ** -0.5

    def gain(k, shape):
        return 1.0 + 0.02 * jax.random.normal(k, shape, jnp.float32)

    return {
        'x': jax.random.normal(ks[0], (BATCH, SEQ, D_MODEL), jnp.float32),
        'norm_gains': gain(ks[1], (DEPTH, N_SUBLAYER_NORMS, D_MODEL)),
        'ffn_w1': w(ks[2], (DEPTH, 2, D_MODEL, D_FF), D_MODEL),
        'ffn_w3': w(ks[3], (DEPTH, 2, D_MODEL, D_FF), D_MODEL),
        'ffn_w2': w(ks[4], (DEPTH, 2, D_FF, D_MODEL), D_FF),
        'moba_w_qkv': w(ks[5], (N_A_LAYERS, D_MODEL, 3 * A_HEADS * A_HEAD_DIM), D_MODEL),
        'moba_w_o': w(ks[6], (N_A_LAYERS, A_HEADS * A_HEAD_DIM, D_MODEL), A_HEADS * A_HEAD_DIM),
        'mla_w_dq': w(ks[7], (N_B_LAYERS, D_MODEL, Q_LORA_RANK), D_MODEL),
        'mla_q_norm': gain(ks[8], (N_B_LAYERS, Q_LORA_RANK)),
        'mla_w_uq': w(ks[9], (N_B_LAYERS, Q_LORA_RANK, B_HEADS * (QK_NOPE_DIM + QK_ROPE_DIM)), Q_LORA_RANK),
        'mla_w_o': w(ks[10], (N_B_LAYERS, B_HEADS * V_HEAD_DIM, D_MODEL), B_HEADS * V_HEAD_DIM),
        'kv_in_norm': gain(ks[11], (D_MODEL,)),
        'w_dkv_kr': w(ks[12], (D_MODEL, KV_LORA_RANK + QK_ROPE_DIM), D_MODEL),
        'kv_norm': gain(ks[13], (KV_LORA_RANK,)),
        'w_ukv': w(ks[14], (KV_LORA_RANK, B_HEADS * (QK_NOPE_DIM + V_HEAD_DIM)), KV_LORA_RANK),
    }


def reference(x, norm_gains, ffn_w1, ffn_w3, ffn_w2, moba_w_qkv, moba_w_o, mla_w_dq, mla_q_norm,
              mla_w_uq, mla_w_o, kv_in_norm, w_dkv_kr, kv_norm, w_ukv):
    s_len = x.shape[1]
    cos_a, sin_a = rope_tables(s_len, A_HEAD_DIM)
    cos_r, sin_r = rope_tables(s_len, QK_ROPE_DIM)
    k_nope = k_rope = v_sh = None
    for l in range(DEPTH):
        g = norm_gains[l]
        x = x + 0.5 * rms_norm(swiglu(rms_norm(x, g[0]), ffn_w1[l, 0], ffn_w3[l, 0], ffn_w2[l, 0]), g[1])
        hn = rms_norm(x, g[2])
        if l < N_A_LAYERS:
            mix = moba_mixer(hn, moba_w_qkv[l], moba_w_o[l], cos_a[:, None, :], sin_a[:, None, :])
        else:
            j = l - N_A_LAYERS
            mix = mla_mixer(hn, mla_w_dq[j], mla_q_norm[j], mla_w_uq[j], mla_w_o[j],
                            k_nope, k_rope, v_sh, cos_r[:, None, :], sin_r[:, None, :])
        x = x + rms_norm(mix, g[3])
        x = x + 0.5 * rms_norm(swiglu(rms_norm(x, g[4]), ffn_w1[l, 1], ffn_w3[l, 1], ffn_w2[l, 1]), g[5])
        if l == N_A_LAYERS - 1:
            k_nope, k_rope, v_sh = mla_shared_kv(x, kv_in_norm, w_dkv_kr, kv_norm, w_ukv, cos_r, sin_r)
    return x
```

```python
import functools

import jax
import jax.numpy as jnp
import numpy as np
from jax import lax
from jax.experimental import pallas as pl
from jax.experimental.pallas import tpu as pltpu

D_MODEL = 1024
N_HEADS = 8
HEAD_DIM = 128
MOBA_BLOCK = 256
MOBA_TOPK = 3
ROPE_DIM = 64
KV_LORA = 256
Q_LORA = 512
ROPE_THETA = 10000.0
NORM_EPS = 1e-6
LANES = 128
MLA_QK_PAD = 2 * LANES
V7X_VMEM_BYTES = 64 * 1024 * 1024
LOG2_E = 1.4426950408889634

F32 = jnp.float32
BF16 = jnp.bfloat16


def _vmem_limit(estimate_bytes):
    return int(min(estimate_bytes * 3 // 2, V7X_VMEM_BYTES * 15 // 16))


def _rms(x, g):
    ms = jnp.mean(x * x, axis=-1, keepdims=True)
    return x * lax.rsqrt(ms + NORM_EPS) * g


def _resident(shape, index_map):
    return pl.BlockSpec(shape, index_map, pipeline_mode=pl.Buffered(1))


def _ffn_kernel(x_ref, g_ref, w1_ref, w3_ref, w2_ref, o_ref, s_ref, *, ff_chunks):
    x = x_ref[...]
    h = _rms(x, g_ref[0:1, :]).astype(BF16)
    for start, size in ff_chunks:
        a = jnp.dot(h, w1_ref[:, start:start + size], preferred_element_type=F32)
        b = jnp.dot(h, w3_ref[:, start:start + size], preferred_element_type=F32)
        silu = a * (1.0 / (1.0 + jnp.exp(-a)))
        s_ref[:, start:start + size] = (silu * b).astype(BF16)
    y = jnp.dot(s_ref[...], w2_ref[...], preferred_element_type=F32)
    o_ref[...] = x + 0.5 * _rms(y, g_ref[1:2, :])


def _ffn(x2d, gains, w1, w3, w2, layer, half, *, tm=512):
    mt, d = x2d.shape
    ff = w1.shape[-1]
    chunk = 4 * LANES
    ff_chunks = tuple((s, min(chunk, ff - s)) for s in range(0, ff, chunk))
    pair = 2 * half
    est = (4 * tm * d * 4 + 3 * d * ff * 2 + tm * ff * 2 + 4 * tm * chunk * 4 + 2 * tm * d * 4)
    return pl.pallas_call(
        functools.partial(_ffn_kernel, ff_chunks=ff_chunks),
        out_shape=jax.ShapeDtypeStruct((mt, d), F32),
        grid=(mt // tm,),
        in_specs=[
            pl.BlockSpec((tm, d), lambda i: (i, 0)),
            _resident((None, None, 2, d), lambda i: (layer, pair, 0, 0)),
            _resident((None, None, d, ff), lambda i: (layer, half, 0, 0)),
            _resident((None, None, d, ff), lambda i: (layer, half, 0, 0)),
            _resident((None, None, ff, d), lambda i: (layer, half, 0, 0)),
        ],
        out_specs=pl.BlockSpec((tm, d), lambda i: (i, 0)),
        scratch_shapes=[pltpu.VMEM((tm, ff), BF16)],
        compiler_params=pltpu.CompilerParams(
            dimension_semantics=("parallel",), vmem_limit_bytes=_vmem_limit(est)),
        name=f"ffn_l{layer}_h{half}",
    )(x2d, gains.reshape(gains.shape[0], 3, 2, d), w1, w3, w2)


def _rope_group(x, cos_t, sin_t):
    return x * cos_t + pltpu.roll(x, LANES // 2, 1) * sin_t


def _moba_qkv_kernel(x_ref, g_ref, w_ref, cos_ref, sin_ref, q_ref, k_ref, vt_ref):
    d = D_MODEL
    h = _rms(x_ref[0], g_ref[...]).astype(BF16)
    cos_t = cos_ref[...]
    sin_t = sin_ref[...]
    q = jnp.dot(h, w_ref[:, 0:d], preferred_element_type=F32)
    k = jnp.dot(h, w_ref[:, d:2 * d], preferred_element_type=F32)
    v = jnp.dot(h, w_ref[:, 2 * d:3 * d], preferred_element_type=F32)
    for hd in range(N_HEADS):
        sl = slice(hd * HEAD_DIM, (hd + 1) * HEAD_DIM)
        q_ref[0, :, sl] = _rope_group(q[:, sl], cos_t, sin_t).astype(BF16)
        k_ref[0, :, sl] = _rope_group(k[:, sl], cos_t, sin_t).astype(BF16)
    vt_ref[0] = v.T.astype(BF16)


def _moba_qkv(x3d, gain, w_qkv, cos_t, sin_t, *, tm=512):
    b, s, d = x3d.shape
    est = 2 * tm * d * 4 + d * 3 * d * 2 + 4 * tm * LANES * 4 + 6 * tm * d * 2 + 4 * tm * d * 4
    return pl.pallas_call(
        _moba_qkv_kernel,
        out_shape=(jax.ShapeDtypeStruct((b, s, d), BF16),
                   jax.ShapeDtypeStruct((b, s, d), BF16),
                   jax.ShapeDtypeStruct((b, d, s), BF16)),
        grid=(b, s // tm),
        in_specs=[
            pl.BlockSpec((1, tm, d), lambda i, j: (i, j, 0)),
            _resident((1, d), lambda i, j: (0, 0)),
            _resident((d, 3 * d), lambda i, j: (0, 0)),
            pl.BlockSpec((tm, LANES), lambda i, j: (j, 0)),
            pl.BlockSpec((tm, LANES), lambda i, j: (j, 0)),
        ],
        out_specs=(pl.BlockSpec((1, tm, d), lambda i, j: (i, j, 0)),
                   pl.BlockSpec((1, tm, d), lambda i, j: (i, j, 0)),
                   pl.BlockSpec((1, d, tm), lambda i, j: (i, 0, j))),
        compiler_params=pltpu.CompilerParams(
            dimension_semantics=("parallel", "parallel"), vmem_limit_bytes=_vmem_limit(est)),
        name="moba_qkv",
    )(x3d, gain.reshape(1, d), w_qkv, cos_t, sin_t)


def _mla_kv_kernel(x_ref, gin_ref, wd_ref, gkv_ref, wu_ref, cos_ref, sin_ref, ka_ref, vt_ref):
    h = _rms(x_ref[0], gin_ref[...]).astype(BF16)
    ckr = jnp.dot(h, wd_ref[...], preferred_element_type=F32)
    c_kv = _rms(ckr[:, :KV_LORA], gkv_ref[...]).astype(BF16)
    k_rope = _rope_group(ckr[:, KV_LORA:], cos_ref[...], sin_ref[...]).astype(BF16)
    kv = jnp.dot(c_kv, wu_ref[...], preferred_element_type=F32)
    for hd in range(N_HEADS):
        base = hd * MLA_QK_PAD
        ka_ref[0, :, base:base + HEAD_DIM] = kv[:, base:base + HEAD_DIM].astype(BF16)
        ka_ref[0, :, base + HEAD_DIM:base + MLA_QK_PAD] = k_rope
        vt_ref[0, hd * HEAD_DIM:(hd + 1) * HEAD_DIM, :] = (
            kv[:, base + HEAD_DIM:base + MLA_QK_PAD].T.astype(BF16))


def _mla_kv(x3d, kv_in_norm, wd_pad, kv_norm, w_ukv, cos_t, sin_t, *, tm=512):
    b, s, d = x3d.shape
    nd = wd_pad.shape[1]
    nu = w_ukv.shape[1]
    est = 2 * tm * d * 4 + d * nd * 2 + KV_LORA * nu * 2 + 2 * tm * nu * 2 + 2 * tm * d * 2 + 3 * tm * nu * 4
    return pl.pallas_call(
        _mla_kv_kernel,
        out_shape=(jax.ShapeDtypeStruct((b, s, N_HEADS * MLA_QK_PAD), BF16),
                   jax.ShapeDtypeStruct((b, d, s), BF16)),
        grid=(b, s // tm),
        in_specs=[
            pl.BlockSpec((1, tm, d), lambda i, j: (i, j, 0)),
            _resident((1, d), lambda i, j: (0, 0)),
            _resident((d, nd), lambda i, j: (0, 0)),
            _resident((1, KV_LORA), lambda i, j: (0, 0)),
            _resident((KV_LORA, nu), lambda i, j: (0, 0)),
            pl.BlockSpec((tm, LANES), lambda i, j: (j, 0)),
            pl.BlockSpec((tm, LANES), lambda i, j: (j, 0)),
        ],
        out_specs=(pl.BlockSpec((1, tm, N_HEADS * MLA_QK_PAD), lambda i, j: (i, j, 0)),
                   pl.BlockSpec((1, d, tm), lambda i, j: (i, 0, j))),
        compiler_params=pltpu.CompilerParams(
            dimension_semantics=("parallel", "parallel"), vmem_limit_bytes=_vmem_limit(est)),
        name="mla_shared_kv",
    )(x3d, kv_in_norm.reshape(1, d), wd_pad, kv_norm.reshape(1, KV_LORA), w_ukv, cos_t, sin_t)


def _mla_q_kernel(x_ref, g_ref, wd_ref, gq_ref, wu_ref, cos_ref, sin_ref, qa_ref):
    h = _rms(x_ref[0], g_ref[...]).astype(BF16)
    cq = jnp.dot(h, wd_ref[...], preferred_element_type=F32)
    cqn = _rms(cq, gq_ref[...]).astype(BF16)
    qa = jnp.dot(cqn, wu_ref[...], preferred_element_type=F32)
    cos_t = cos_ref[...]
    sin_t = sin_ref[...]
    for hd in range(N_HEADS):
        base = hd * MLA_QK_PAD
        qa_ref[0, :, base:base + HEAD_DIM] = qa[:, base:base + HEAD_DIM].astype(BF16)
        qa_ref[0, :, base + HEAD_DIM:base + MLA_QK_PAD] = _rope_group(
            qa[:, base + HEAD_DIM:base + MLA_QK_PAD], cos_t, sin_t).astype(BF16)


def _mla_q(x3d, gain, w_dq, q_norm, wu_pad, cos_t, sin_t, *, tm=512):
    b, s, d = x3d.shape
    nu = wu_pad.shape[1]
    est = 2 * tm * d * 4 + d * Q_LORA * 2 + Q_LORA * nu * 2 + 2 * tm * nu * 2 + 3 * tm * nu * 4
    return pl.pallas_call(
        _mla_q_kernel,
        out_shape=jax.ShapeDtypeStruct((b, s, nu), BF16),
        grid=(b, s // tm),
        in_specs=[
            pl.BlockSpec((1, tm, d), lambda i, j: (i, j, 0)),
            _resident((1, d), lambda i, j: (0, 0)),
            _resident((d, Q_LORA), lambda i, j: (0, 0)),
            _resident((1, Q_LORA), lambda i, j: (0, 0)),
            _resident((Q_LORA, nu), lambda i, j: (0, 0)),
            pl.BlockSpec((tm, LANES), lambda i, j: (j, 0)),
            pl.BlockSpec((tm, LANES), lambda i, j: (j, 0)),
        ],
        out_specs=pl.BlockSpec((1, tm, nu), lambda i, j: (i, j, 0)),
        compiler_params=pltpu.CompilerParams(
            dimension_semantics=("parallel", "parallel"), vmem_limit_bytes=_vmem_limit(est)),
        name="mla_q",
    )(x3d, gain.reshape(1, d), w_dq, q_norm.reshape(1, Q_LORA), wu_pad, cos_t, sin_t)


def _moba_select_bias(gate, own):
    rows = [gate[n:n + 1, :] for n in range(own)]
    biases = []
    for n in range(own):
        rank = jnp.zeros_like(rows[n])
        for m in range(own):
            if m < n:
                rank = rank + jnp.where(rows[m] >= rows[n], 1.0, 0.0)
            elif m > n:
                rank = rank + jnp.where(rows[m] > rows[n], 1.0, 0.0)
        biases.append(jnp.where(rank < float(MOBA_TOPK), 0.0, -jnp.inf))
    return biases


def _attn_kernel(q_ref, k_ref, vt_ref, o_ref, *, moba, exp_scale):
    s_len = q_ref.shape[1]
    blk = MOBA_BLOCK
    nb = s_len // blk
    nt = (((1,), (1,)), ((), ()))
    key_i = lax.broadcasted_iota(jnp.int32, (blk, blk), 0)
    qry_i = lax.broadcasted_iota(jnp.int32, (blk, blk), 1)
    causal_bias = jnp.where(key_i <= qry_i, 0.0, -jnp.inf)

    if moba:
        sub_i = lax.broadcasted_iota(jnp.int32, (8, HEAD_DIM), 0)
        km = jnp.zeros((8, HEAD_DIM), F32)
        for n in range(nb - 1):
            row = jnp.sum(k_ref[0, n * blk:(n + 1) * blk, :].astype(F32), axis=0, keepdims=True)
            km = jnp.where(sub_i == n, row * (1.0 / blk), km)
        km_hi = km.astype(BF16)
        km_lo = (km - km_hi.astype(F32)).astype(BF16)
        km_hl = jnp.concatenate([km_hi, km_lo], axis=0)

    for qi in range(nb):
        q_blk = q_ref[0, qi * blk:(qi + 1) * blk, :]
        sel_bias = None
        if moba and qi > MOBA_TOPK:
            g2 = lax.dot_general(km_hl, q_blk, nt, preferred_element_type=F32)
            sel_bias = _moba_select_bias(g2[0:8, :] + g2[8:16, :], qi)
        s_list = []
        m = None
        for n in range(qi + 1):
            k_blk = k_ref[0, n * blk:(n + 1) * blk, :]
            s = lax.dot_general(k_blk, q_blk, nt, preferred_element_type=F32)
            if n == qi:
                s = s + causal_bias
            elif sel_bias is not None:
                s = s + sel_bias[n]
            s_list.append(s)
            bm = jnp.max(s, axis=0, keepdims=True)
            m = bm if m is None else jnp.maximum(m, bm)
        p_list = []
        l = None
        for s in s_list:
            p = jnp.exp2((s - m) * exp_scale)
            ps = jnp.sum(p, axis=0, keepdims=True)
            l = ps if l is None else l + ps
            p_list.append(p.astype(BF16))
        p_t = p_list[0] if len(p_list) == 1 else jnp.concatenate(p_list, axis=0)
        out_t = jnp.dot(vt_ref[0, :, 0:(qi + 1) * blk], p_t, preferred_element_type=F32)
        out_t = out_t * (1.0 / l)
        o_ref[0, qi * blk:(qi + 1) * blk, :] = out_t.T.astype(BF16)


def _attention(q, k, vt, *, moba, qk_dim, scale):
    b, s, _ = q.shape
    dv = HEAD_DIM
    est = 2 * (2 * s * qk_dim * 2 + 2 * s * dv * 2) + 24 * MOBA_BLOCK * s * 4
    return pl.pallas_call(
        functools.partial(_attn_kernel, moba=moba, exp_scale=scale * LOG2_E),
        out_shape=jax.ShapeDtypeStruct((b, s, N_HEADS * dv), BF16),
        grid=(b, N_HEADS),
        in_specs=[
            pl.BlockSpec((1, s, qk_dim), lambda i, j: (i, 0, j)),
            pl.BlockSpec((1, s, qk_dim), lambda i, j: (i, 0, j)),
            pl.BlockSpec((1, dv, s), lambda i, j: (i, j, 0)),
        ],
        out_specs=pl.BlockSpec((1, s, dv), lambda i, j: (i, 0, j)),
        compiler_params=pltpu.CompilerParams(
            dimension_semantics=("parallel", "parallel"), vmem_limit_bytes=_vmem_limit(est)),
        name="moba_attn" if moba else "mla_attn",
    )(q, k, vt)


def _oproj_kernel(x_ref, o_ref, w_ref, g_ref, y_ref):
    mix = jnp.dot(o_ref[...], w_ref[...], preferred_element_type=F32)
    y_ref[...] = x_ref[...] + _rms(mix, g_ref[...])


def _oproj(x2d, o2d, w_o, gain, *, tm=512):
    mt, d = x2d.shape
    est = 4 * tm * d * 4 + 2 * tm * d * 2 + d * d * 2 + 2 * tm * d * 4
    return pl.pallas_call(
        _oproj_kernel,
        out_shape=jax.ShapeDtypeStruct((mt, d), F32),
        grid=(mt // tm,),
        in_specs=[
            pl.BlockSpec((tm, d), lambda i: (i, 0)),
            pl.BlockSpec((tm, d), lambda i: (i, 0)),
            _resident((d, d), lambda i: (0, 0)),
            _resident((1, d), lambda i: (0, 0)),
        ],
        out_specs=pl.BlockSpec((tm, d), lambda i: (i, 0)),
        compiler_params=pltpu.CompilerParams(
            dimension_semantics=("parallel",), vmem_limit_bytes=_vmem_limit(est)),
        name="mixer_oproj",
    )(x2d, o2d, w_o, gain.reshape(1, d))


def _rope_tables(seq_len, dim):
    inv_freq = 1.0 / (ROPE_THETA ** (jnp.arange(0, dim, 2, dtype=F32) / dim))
    ang = jnp.arange(seq_len, dtype=F32)[:, None] * inv_freq[None, :]
    return jnp.cos(ang), jnp.sin(ang)


def _mla_rope_group_cols():
    half = ROPE_DIM // 2
    return np.concatenate([np.arange(half), LANES // 2 + np.arange(half)])


def kernel(x, norm_gains, ffn_w1, ffn_w3, ffn_w2, moba_w_qkv, moba_w_o, mla_w_dq, mla_q_norm,
           mla_w_uq, mla_w_o, kv_in_norm, w_dkv_kr, kv_norm, w_ukv):
    b, s, d = x.shape
    depth = norm_gains.shape[0]
    n_a = moba_w_qkv.shape[0]

    cos_a, sin_a = _rope_tables(s, HEAD_DIM)
    cos_moba = jnp.concatenate([cos_a, cos_a], axis=1)
    sin_moba = jnp.concatenate([-sin_a, sin_a], axis=1)
    cos_r, sin_r = _rope_tables(s, ROPE_DIM)
    zpad = jnp.zeros_like(cos_r)
    cos_mla = jnp.concatenate([cos_r, zpad, cos_r, zpad], axis=1)
    sin_mla = jnp.concatenate([-sin_r, zpad, sin_r, zpad], axis=1)

    w1 = ffn_w1.astype(BF16)
    w3 = ffn_w3.astype(BF16)
    w2 = ffn_w2.astype(BF16)
    w_qkv = moba_w_qkv.astype(BF16)
    w_o_a = moba_w_o.astype(BF16)
    w_o_b = mla_w_o.astype(BF16)
    w_dq = mla_w_dq.astype(BF16)
    grp = _mla_rope_group_cols()
    wd_pad = jnp.zeros((d, KV_LORA + LANES), F32)
    wd_pad = wd_pad.at[:, :KV_LORA].set(w_dkv_kr[:, :KV_LORA])
    wd_pad = wd_pad.at[:, KV_LORA + grp].set(w_dkv_kr[:, KV_LORA:]).astype(BF16)
    uq = mla_w_uq.reshape(mla_w_uq.shape[0], Q_LORA, N_HEADS, HEAD_DIM + ROPE_DIM)
    wu_pad = jnp.zeros((mla_w_uq.shape[0], Q_LORA, N_HEADS, MLA_QK_PAD), F32)
    wu_pad = wu_pad.at[..., :HEAD_DIM].set(uq[..., :HEAD_DIM])
    wu_pad = wu_pad.at[..., HEAD_DIM + grp].set(uq[..., HEAD_DIM:])
    wu_pad = wu_pad.reshape(mla_w_uq.shape[0], Q_LORA, N_HEADS * MLA_QK_PAD).astype(BF16)
    w_ukv_b = w_ukv.astype(BF16)

    x2 = x.reshape(b * s, d)
    k_aug = vt_shared = None
    for l in range(depth):
        g = norm_gains[l]
        x2 = _ffn(x2, norm_gains, w1, w3, w2, l, 0)
        x3 = x2.reshape(b, s, d)
        if l < n_a:
            q, k, vt = _moba_qkv(x3, g[2], w_qkv[l], cos_moba, sin_moba)
            o = _attention(q, k, vt, moba=True, qk_dim=HEAD_DIM, scale=HEAD_DIM ** -0.5)
            x2 = _oproj(x2, o.reshape(b * s, d), w_o_a[l], g[3])
        else:
            j = l - n_a
            q_aug = _mla_q(x3, g[2], w_dq[j], mla_q_norm[j], wu_pad[j], cos_mla, sin_mla)
            o = _attention(q_aug, k_aug, vt_shared, moba=False, qk_dim=MLA_QK_PAD,
                           scale=(HEAD_DIM + ROPE_DIM) ** -0.5)
            x2 = _oproj(x2, o.reshape(b * s, d), w_o_b[j], g[3])
        x2 = _ffn(x2, norm_gains, w1, w3, w2, l, 1)
        if l == n_a - 1:
            k_aug, vt_shared = _mla_kv(x2.reshape(b, s, d), kv_in_norm, wd_pad, kv_norm,
                                       w_ukv_b, cos_mla, sin_mla)
    return x2.reshape(b, s, d)
```

```python
import functools

import jax
import jax.numpy as jnp
from jax import lax
from jax.experimental import pallas as pl
from jax.experimental.pallas import tpu as pltpu

D_MODEL = 1024
N_HEADS = 8
HEAD_DIM = 128
MOBA_BLOCK = 256
MOBA_TOPK = 3
ROPE_DIM = 64
KV_LORA = 256
Q_LORA = 512
ROPE_THETA = 10000.0
NORM_EPS = 1e-6
LANES = 128
MLA_QK_PAD = 2 * LANES
SUM_ROWS = 16
ATTN_LOOKAHEAD = 3
V7X_VMEM_BYTES = 64 * 1024 * 1024
LOG2_E = 1.4426950408889634

F32 = jnp.float32
BF16 = jnp.bfloat16


def _vmem_limit(estimate_bytes):
    return int(min(estimate_bytes * 3 // 2, V7X_VMEM_BYTES * 15 // 16))


def _rms(x, g):
    ms = jnp.mean(x * x, axis=-1, keepdims=True)
    return x * lax.rsqrt(ms + NORM_EPS) * g


def _resident(shape, index_map):
    return pl.BlockSpec(shape, index_map, pipeline_mode=pl.Buffered(1))


def _ffn_kernel(x_ref, g_ref, w1_ref, w3_ref, w2_ref, o_ref, s_ref, *, ff_chunks):
    x = x_ref[...]
    h = _rms(x, g_ref[0:1, :]).astype(BF16)
    for start, size in ff_chunks:
        a = jnp.dot(h, w1_ref[:, start:start + size], preferred_element_type=F32)
        b = jnp.dot(h, w3_ref[:, start:start + size], preferred_element_type=F32)
        silu = a * (1.0 / (1.0 + jnp.exp(-a)))
        s_ref[:, start:start + size] = (silu * b).astype(BF16)
    y = jnp.dot(s_ref[...], w2_ref[...], preferred_element_type=F32)
    o_ref[...] = x + 0.5 * _rms(y, g_ref[1:2, :])


def _ffn(x2d, gains, w1, w3, w2, layer, half, *, tm=512):
    mt, d = x2d.shape
    ff = w1.shape[-1]
    chunk = 4 * LANES
    ff_chunks = tuple((s, min(chunk, ff - s)) for s in range(0, ff, chunk))
    pair = 2 * half
    est = (4 * tm * d * 4 + 3 * d * ff * 2 + tm * ff * 2 + 4 * tm * chunk * 4 + 2 * tm * d * 4)
    return pl.pallas_call(
        functools.partial(_ffn_kernel, ff_chunks=ff_chunks),
        out_shape=jax.ShapeDtypeStruct((mt, d), F32),
        grid=(mt // tm,),
        in_specs=[
            pl.BlockSpec((tm, d), lambda i: (i, 0)),
            _resident((None, None, 2, d), lambda i: (layer, pair, 0, 0)),
            _resident((None, None, d, ff), lambda i: (layer, half, 0, 0)),
            _resident((None, None, d, ff), lambda i: (layer, half, 0, 0)),
            _resident((None, None, ff, d), lambda i: (layer, half, 0, 0)),
        ],
        out_specs=pl.BlockSpec((tm, d), lambda i: (i, 0)),
        scratch_shapes=[pltpu.VMEM((tm, ff), BF16)],
        compiler_params=pltpu.CompilerParams(
            dimension_semantics=("parallel",), vmem_limit_bytes=_vmem_limit(est)),
        name=f"ffn_l{layer}_h{half}",
    )(x2d, gains.reshape(gains.shape[0], 3, 2, d), w1, w3, w2)


def _rope_group(x, cos_t, sin_t):
    return x * cos_t + pltpu.roll(x, LANES // 2, 1) * sin_t


def _moba_qkv_kernel(x_ref, g_ref, w_ref, cos_ref, sin_ref, q_ref, k_ref, vt_ref):
    d = D_MODEL
    h = _rms(x_ref[0], g_ref[...]).astype(BF16)
    cos_t = cos_ref[...]
    sin_t = sin_ref[...]
    q = jnp.dot(h, w_ref[:, 0:d], preferred_element_type=F32)
    k = jnp.dot(h, w_ref[:, d:2 * d], preferred_element_type=F32)
    v = jnp.dot(h, w_ref[:, 2 * d:3 * d], preferred_element_type=F32)
    for hd in range(N_HEADS):
        sl = slice(hd * HEAD_DIM, (hd + 1) * HEAD_DIM)
        q_ref[0, :, sl] = _rope_group(q[:, sl], cos_t, sin_t).astype(BF16)
        k_ref[0, :, sl] = _rope_group(k[:, sl], cos_t, sin_t).astype(BF16)
    vt_ref[0] = v.T.astype(BF16)


def _moba_qkv(x3d, gain, w_qkv, layer, cos_t, sin_t, *, tm=512):
    b, s, d = x3d.shape
    est = 2 * tm * d * 4 + d * 3 * d * 2 + 4 * tm * LANES * 4 + 6 * tm * d * 2 + 4 * tm * d * 4
    return pl.pallas_call(
        _moba_qkv_kernel,
        out_shape=(jax.ShapeDtypeStruct((b, s, d), BF16),
                   jax.ShapeDtypeStruct((b, s, d), BF16),
                   jax.ShapeDtypeStruct((b, d, s), BF16)),
        grid=(b, s // tm),
        in_specs=[
            pl.BlockSpec((1, tm, d), lambda i, j: (i, j, 0)),
            _resident((1, d), lambda i, j: (0, 0)),
            _resident((None, d, 3 * d), lambda i, j: (layer, 0, 0)),
            pl.BlockSpec((tm, LANES), lambda i, j: (j, 0)),
            pl.BlockSpec((tm, LANES), lambda i, j: (j, 0)),
        ],
        out_specs=(pl.BlockSpec((1, tm, d), lambda i, j: (i, j, 0)),
                   pl.BlockSpec((1, tm, d), lambda i, j: (i, j, 0)),
                   pl.BlockSpec((1, d, tm), lambda i, j: (i, 0, j))),
        compiler_params=pltpu.CompilerParams(
            dimension_semantics=("parallel", "parallel"), vmem_limit_bytes=_vmem_limit(est)),
        name="moba_qkv",
    )(x3d, gain.reshape(1, d), w_qkv, cos_t, sin_t)


def _mla_kv_kernel(x_ref, gin_ref, wd_ref, gkv_ref, wu_ref, cos_ref, sin_ref, ka_ref, vt_ref):
    h = _rms(x_ref[0], gin_ref[...]).astype(BF16)
    ckr = jnp.dot(h, wd_ref[...], preferred_element_type=F32)
    c_kv = _rms(ckr[:, :KV_LORA], gkv_ref[...]).astype(BF16)
    k_rope = _rope_group(ckr[:, KV_LORA:], cos_ref[...], sin_ref[...]).astype(BF16)
    kv = jnp.dot(c_kv, wu_ref[...], preferred_element_type=F32)
    for hd in range(N_HEADS):
        base = hd * MLA_QK_PAD
        ka_ref[0, :, base:base + HEAD_DIM] = kv[:, base:base + HEAD_DIM].astype(BF16)
        ka_ref[0, :, base + HEAD_DIM:base + MLA_QK_PAD] = k_rope
        vt_ref[0, hd * HEAD_DIM:(hd + 1) * HEAD_DIM, :] = (
            kv[:, base + HEAD_DIM:base + MLA_QK_PAD].T.astype(BF16))


def _mla_kv(x3d, kv_in_norm, wd_pad, kv_norm, w_ukv, cos_t, sin_t, *, tm=512):
    b, s, d = x3d.shape
    nd = wd_pad.shape[1]
    nu = w_ukv.shape[1]
    est = 2 * tm * d * 4 + d * nd * 2 + KV_LORA * nu * 2 + 2 * tm * nu * 2 + 2 * tm * d * 2 + 3 * tm * nu * 4
    return pl.pallas_call(
        _mla_kv_kernel,
        out_shape=(jax.ShapeDtypeStruct((b, s, N_HEADS * MLA_QK_PAD), BF16),
                   jax.ShapeDtypeStruct((b, d, s), BF16)),
        grid=(b, s // tm),
        in_specs=[
            pl.BlockSpec((1, tm, d), lambda i, j: (i, j, 0)),
            _resident((1, d), lambda i, j: (0, 0)),
            _resident((d, nd), lambda i, j: (0, 0)),
            _resident((1, KV_LORA), lambda i, j: (0, 0)),
            _resident((KV_LORA, nu), lambda i, j: (0, 0)),
            pl.BlockSpec((tm, LANES), lambda i, j: (j, 0)),
            pl.BlockSpec((tm, LANES), lambda i, j: (j, 0)),
        ],
        out_specs=(pl.BlockSpec((1, tm, N_HEADS * MLA_QK_PAD), lambda i, j: (i, j, 0)),
                   pl.BlockSpec((1, d, tm), lambda i, j: (i, 0, j))),
        compiler_params=pltpu.CompilerParams(
            dimension_semantics=("parallel", "parallel"), vmem_limit_bytes=_vmem_limit(est)),
        name="mla_shared_kv",
    )(x3d, kv_in_norm.reshape(1, d), wd_pad, kv_norm.reshape(1, KV_LORA), w_ukv, cos_t, sin_t)


def _mla_q_kernel(x_ref, g_ref, wd_ref, gq_ref, wu_ref, cos_ref, sin_ref, qa_ref):
    h = _rms(x_ref[0], g_ref[...]).astype(BF16)
    cq = jnp.dot(h, wd_ref[...], preferred_element_type=F32)
    cqn = _rms(cq, gq_ref[...]).astype(BF16)
    qa = jnp.dot(cqn, wu_ref[...], preferred_element_type=F32)
    cos_t = cos_ref[...]
    sin_t = sin_ref[...]
    for hd in range(N_HEADS):
        base = hd * MLA_QK_PAD
        qa_ref[0, :, base:base + HEAD_DIM] = qa[:, base:base + HEAD_DIM].astype(BF16)
        qa_ref[0, :, base + HEAD_DIM:base + MLA_QK_PAD] = _rope_group(
            qa[:, base + HEAD_DIM:base + MLA_QK_PAD], cos_t, sin_t).astype(BF16)


def _mla_q(x3d, gain, w_dq, q_norm, wu_pad, layer, cos_t, sin_t, *, tm=512):
    b, s, d = x3d.shape
    nu = wu_pad.shape[-1]
    est = 2 * tm * d * 4 + d * Q_LORA * 2 + Q_LORA * nu * 2 + 2 * tm * nu * 2 + 3 * tm * nu * 4
    return pl.pallas_call(
        _mla_q_kernel,
        out_shape=jax.ShapeDtypeStruct((b, s, nu), BF16),
        grid=(b, s // tm),
        in_specs=[
            pl.BlockSpec((1, tm, d), lambda i, j: (i, j, 0)),
            _resident((1, d), lambda i, j: (0, 0)),
            _resident((None, d, Q_LORA), lambda i, j: (layer, 0, 0)),
            _resident((1, Q_LORA), lambda i, j: (0, 0)),
            _resident((None, Q_LORA, nu), lambda i, j: (layer, 0, 0)),
            pl.BlockSpec((tm, LANES), lambda i, j: (j, 0)),
            pl.BlockSpec((tm, LANES), lambda i, j: (j, 0)),
        ],
        out_specs=pl.BlockSpec((1, tm, nu), lambda i, j: (i, j, 0)),
        compiler_params=pltpu.CompilerParams(
            dimension_semantics=("parallel", "parallel"), vmem_limit_bytes=_vmem_limit(est)),
        name="mla_q",
    )(x3d, gain.reshape(1, d), w_dq, q_norm.reshape(1, Q_LORA), wu_pad, cos_t, sin_t)


def _moba_select_bias(gate, own):
    rows = [gate[n:n + 1, :] for n in range(own)]
    biases = []
    for n in range(own):
        rank = jnp.zeros_like(rows[n])
        for m in range(own):
            if m < n:
                rank = rank + jnp.where(rows[m] >= rows[n], 1.0, 0.0)
            elif m > n:
                rank = rank + jnp.where(rows[m] > rows[n], 1.0, 0.0)
        biases.append(jnp.where(rank < float(MOBA_TOPK), 0.0, -jnp.inf))
    return biases


def _attn_kernel(q_ref, k_ref, vt_ref, o_ref, *, moba, exp_scale):
    s_len = q_ref.shape[1]
    blk = MOBA_BLOCK
    nb = s_len // blk
    nt = (((1,), (1,)), ((), ()))
    key_i = lax.broadcasted_iota(jnp.int32, (blk, blk), 0)
    qry_i = lax.broadcasted_iota(jnp.int32, (blk, blk), 1)
    causal_bias = jnp.where(key_i <= qry_i, 0.0, -jnp.inf)

    if moba:
        sub_i = lax.broadcasted_iota(jnp.int32, (8, HEAD_DIM), 0)
        km = jnp.zeros((8, HEAD_DIM), F32)
        for n in range(nb - 1):
            row = jnp.sum(k_ref[0, n * blk:(n + 1) * blk, :].astype(F32), axis=0, keepdims=True)
            km = jnp.where(sub_i == n, row * (1.0 / blk), km)
        km_hi = km.astype(BF16)
        km_lo = (km - km_hi.astype(F32)).astype(BF16)
        km_hl = jnp.concatenate([km_hi, km_lo], axis=0)

    dv = vt_ref.shape[1]
    ones_rows = jnp.ones((SUM_ROWS, s_len), BF16)

    def logits_begin(qi):
        q_blk = q_ref[0, qi * blk:(qi + 1) * blk, :]
        sel_bias = None
        if moba and qi > MOBA_TOPK:
            g2 = lax.dot_general(km_hl, q_blk, nt, preferred_element_type=F32)
            sel_bias = _moba_select_bias(g2[0:8, :] + g2[8:16, :], qi)
        return dict(qi=qi, q=q_blk, bias=sel_bias, s=[], m8=None)

    def logits_step(st, n):
        qi = st["qi"]
        k_blk = k_ref[0, n * blk:(n + 1) * blk, :]
        s = lax.dot_general(k_blk, st["q"], nt, preferred_element_type=F32)
        if n == qi:
            s = s + causal_bias
        st["s"].append(s)
        bm8 = jnp.max(s.reshape(blk // 8, 8, blk), axis=0)
        if n < qi and st["bias"] is not None:
            bm8 = bm8 + st["bias"][n]
        st["m8"] = bm8 if st["m8"] is None else jnp.maximum(st["m8"], bm8)

    def weights_step(st, m, n):
        m_n = m - st["bias"][n] if (n < st["qi"] and st["bias"] is not None) else m
        return jnp.exp2((st["s"][n] - m_n) * exp_scale).astype(BF16)

    def values(qi, p_list):
        p_t = p_list[0] if len(p_list) == 1 else jnp.concatenate(p_list, axis=0)
        kv = (qi + 1) * blk
        vt_aug = jnp.concatenate([vt_ref[0, :, 0:kv], ones_rows[:, 0:kv]], axis=0)
        out_aug = jnp.dot(vt_aug, p_t, preferred_element_type=F32)
        out_t = out_aug[0:dv, :] * (1.0 / out_aug[dv:dv + 1, :])
        o_ref[0, qi * blk:(qi + 1) * blk, :] = out_t.T.astype(BF16)

    def logits_all(qi):
        st = logits_begin(qi)
        for n in range(qi + 1):
            logits_step(st, n)
        return st

    ahead = [logits_all(qi) for qi in range(min(ATTN_LOOKAHEAD, nb))]
    for qi in range(nb):
        if qi + ATTN_LOOKAHEAD < nb:
            ahead.append(logits_all(qi + ATTN_LOOKAHEAD))
        cur = ahead.pop(0)
        m = jnp.max(cur["m8"], axis=0, keepdims=True)
        values(qi, [weights_step(cur, m, n) for n in range(qi + 1)])


def _attention(q, k, vt, *, moba, qk_dim, scale):
    b, s, _ = q.shape
    dv = HEAD_DIM
    est = 2 * (2 * s * qk_dim * 2 + 2 * s * dv * 2) + 24 * MOBA_BLOCK * s * 4
    return pl.pallas_call(
        functools.partial(_attn_kernel, moba=moba, exp_scale=scale * LOG2_E),
        out_shape=jax.ShapeDtypeStruct((b, s, N_HEADS * dv), BF16),
        grid=(b, N_HEADS),
        in_specs=[
            pl.BlockSpec((1, s, qk_dim), lambda i, j: (i, 0, j)),
            pl.BlockSpec((1, s, qk_dim), lambda i, j: (i, 0, j)),
            pl.BlockSpec((1, dv, s), lambda i, j: (i, j, 0)),
        ],
        out_specs=pl.BlockSpec((1, s, dv), lambda i, j: (i, 0, j)),
        compiler_params=pltpu.CompilerParams(
            dimension_semantics=("parallel", "parallel"), vmem_limit_bytes=_vmem_limit(est)),
        name="moba_attn" if moba else "mla_attn",
    )(q, k, vt)


def _oproj_kernel(x_ref, o_ref, w_ref, g_ref, y_ref):
    mix = jnp.dot(o_ref[...], w_ref[...], preferred_element_type=F32)
    y_ref[...] = x_ref[...] + _rms(mix, g_ref[...])


def _oproj(x2d, o2d, w_o, layer, gain, *, tm=512):
    mt, d = x2d.shape
    est = 4 * tm * d * 4 + 2 * tm * d * 2 + d * d * 2 + 2 * tm * d * 4
    return pl.pallas_call(
        _oproj_kernel,
        out_shape=jax.ShapeDtypeStruct((mt, d), F32),
        grid=(mt // tm,),
        in_specs=[
            pl.BlockSpec((tm, d), lambda i: (i, 0)),
            pl.BlockSpec((tm, d), lambda i: (i, 0)),
            _resident((None, d, d), lambda i: (layer, 0, 0)),
            _resident((1, d), lambda i: (0, 0)),
        ],
        out_specs=pl.BlockSpec((tm, d), lambda i: (i, 0)),
        compiler_params=pltpu.CompilerParams(
            dimension_semantics=("parallel",), vmem_limit_bytes=_vmem_limit(est)),
        name="mixer_oproj",
    )(x2d, o2d, w_o, gain.reshape(1, d))


def _rope_tables(seq_len, dim):
    inv_freq = 1.0 / (ROPE_THETA ** (jnp.arange(0, dim, 2, dtype=F32) / dim))
    ang = jnp.arange(seq_len, dtype=F32)[:, None] * inv_freq[None, :]
    return jnp.cos(ang), jnp.sin(ang)


def _rope_group_layout(main, rope):
    half = ROPE_DIM // 2
    zeros = jnp.zeros(rope.shape[:-1] + (LANES // 2 - half,), rope.dtype)
    return jnp.concatenate([main, rope[..., :half], zeros, rope[..., half:], zeros], axis=-1)


def kernel(x, norm_gains, ffn_w1, ffn_w3, ffn_w2, moba_w_qkv, moba_w_o, mla_w_dq, mla_q_norm,
           mla_w_uq, mla_w_o, kv_in_norm, w_dkv_kr, kv_norm, w_ukv):
    b, s, d = x.shape
    depth = norm_gains.shape[0]
    n_a = moba_w_qkv.shape[0]

    cos_a, sin_a = _rope_tables(s, HEAD_DIM)
    cos_moba = jnp.concatenate([cos_a, cos_a], axis=1)
    sin_moba = jnp.concatenate([-sin_a, sin_a], axis=1)
    cos_r, sin_r = _rope_tables(s, ROPE_DIM)
    zpad = jnp.zeros_like(cos_r)
    cos_mla = jnp.concatenate([cos_r, zpad, cos_r, zpad], axis=1)
    sin_mla = jnp.concatenate([-sin_r, zpad, sin_r, zpad], axis=1)

    w1 = ffn_w1.astype(BF16)
    w3 = ffn_w3.astype(BF16)
    w2 = ffn_w2.astype(BF16)
    w_qkv = moba_w_qkv.astype(BF16)
    w_o_a = moba_w_o.astype(BF16)
    w_o_b = mla_w_o.astype(BF16)
    w_dq = mla_w_dq.astype(BF16)
    wd_pad = _rope_group_layout(w_dkv_kr[:, :KV_LORA], w_dkv_kr[:, KV_LORA:]).astype(BF16)
    uq = mla_w_uq.reshape(mla_w_uq.shape[0], Q_LORA, N_HEADS, HEAD_DIM + ROPE_DIM)
    wu_pad = _rope_group_layout(uq[..., :HEAD_DIM], uq[..., HEAD_DIM:])
    wu_pad = wu_pad.reshape(mla_w_uq.shape[0], Q_LORA, N_HEADS * MLA_QK_PAD).astype(BF16)
    w_ukv_b = w_ukv.astype(BF16)

    x2 = x.reshape(b * s, d)
    k_aug = vt_shared = None
    for l in range(depth):
        g = norm_gains[l]
        x2 = _ffn(x2, norm_gains, w1, w3, w2, l, 0)
        x3 = x2.reshape(b, s, d)
        if l < n_a:
            q, k, vt = _moba_qkv(x3, g[2], w_qkv, l, cos_moba, sin_moba)
            o = _attention(q, k, vt, moba=True, qk_dim=HEAD_DIM, scale=HEAD_DIM ** -0.5)
            x2 = _oproj(x2, o.reshape(b * s, d), w_o_a, l, g[3])
        else:
            j = l - n_a
            q_aug = _mla_q(x3, g[2], w_dq, mla_q_norm[j], wu_pad, j, cos_mla, sin_mla)
            o = _attention(q_aug, k_aug, vt_shared, moba=False, qk_dim=MLA_QK_PAD,
                           scale=(HEAD_DIM + ROPE_DIM) ** -0.5)
            x2 = _oproj(x2, o.reshape(b * s, d), w_o_b, j, g[3])
        x2 = _ffn(x2, norm_gains, w1, w3, w2, l, 1)
        if l == n_a - 1:
            k_aug, vt_shared = _mla_kv(x2.reshape(b, s, d), kv_in_norm, wd_pad, kv_norm,
                                       w_ukv_b, cos_mla, sin_mla)
    return x2.reshape(b, s, d)
```

```python
import functools

import jax
import jax.numpy as jnp
from jax import lax
from jax.experimental import pallas as pl
from jax.experimental.pallas import tpu as pltpu

D_MODEL = 1024
N_HEADS = 8
HEAD_DIM = 128
MOBA_BLOCK = 256
MOBA_TOPK = 3
ROPE_DIM = 64
KV_LORA = 256
Q_LORA = 512
ROPE_THETA = 10000.0
NORM_EPS = 1e-6
LANES = 128
MLA_QK_PAD = 2 * LANES
SUM_ROWS = 16
ATTN_LOOKAHEAD = 3
DENOM_LOG2_RANGE = 60.0
V7X_VMEM_BYTES = 64 * 1024 * 1024
LOG2_E = 1.4426950408889634

F32 = jnp.float32
BF16 = jnp.bfloat16


def _vmem_limit(estimate_bytes):
    return int(min(estimate_bytes * 3 // 2, V7X_VMEM_BYTES * 15 // 16))


def _rms(x, g):
    ms = jnp.mean(x * x, axis=-1, keepdims=True)
    return x * lax.rsqrt(ms + NORM_EPS) * g


def _resident(shape, index_map):
    return pl.BlockSpec(shape, index_map, pipeline_mode=pl.Buffered(1))


def _ffn_kernel(x_ref, g_ref, w1_ref, w3_ref, w2_ref, o_ref, s_ref, *, ff_chunks):
    x = x_ref[...]
    h = _rms(x, g_ref[0:1, :]).astype(BF16)
    for start, size in ff_chunks:
        a = jnp.dot(h, w1_ref[:, start:start + size], preferred_element_type=F32)
        b = jnp.dot(h, w3_ref[:, start:start + size], preferred_element_type=F32)
        silu = a * (1.0 / (1.0 + jnp.exp(-a)))
        s_ref[:, start:start + size] = (silu * b).astype(BF16)
    y = jnp.dot(s_ref[...], w2_ref[...], preferred_element_type=F32)
    o_ref[...] = x + 0.5 * _rms(y, g_ref[1:2, :])


def _ffn(x2d, gains, w1, w3, w2, layer, half, *, tm=1024):
    mt, d = x2d.shape
    ff = w1.shape[-1]
    chunk = 4 * LANES
    ff_chunks = tuple((s, min(chunk, ff - s)) for s in range(0, ff, chunk))
    pair = 2 * half
    est = (4 * tm * d * 4 + 3 * d * ff * 2 + tm * ff * 2 + 4 * tm * chunk * 4 + 2 * tm * d * 4)
    return pl.pallas_call(
        functools.partial(_ffn_kernel, ff_chunks=ff_chunks),
        out_shape=jax.ShapeDtypeStruct((mt, d), F32),
        grid=(mt // tm,),
        in_specs=[
            pl.BlockSpec((tm, d), lambda i: (i, 0)),
            _resident((None, None, 2, d), lambda i: (layer, pair, 0, 0)),
            _resident((None, None, d, ff), lambda i: (layer, half, 0, 0)),
            _resident((None, None, d, ff), lambda i: (layer, half, 0, 0)),
            _resident((None, None, ff, d), lambda i: (layer, half, 0, 0)),
        ],
        out_specs=pl.BlockSpec((tm, d), lambda i: (i, 0)),
        scratch_shapes=[pltpu.VMEM((tm, ff), BF16)],
        compiler_params=pltpu.CompilerParams(
            dimension_semantics=("parallel",), vmem_limit_bytes=_vmem_limit(est)),
        name=f"ffn_l{layer}_h{half}",
    )(x2d, gains.reshape(gains.shape[0], 3, 2, d), w1, w3, w2)


def _rope_group(x, cos_t, sin_t):
    return x * cos_t + pltpu.roll(x, LANES // 2, 1) * sin_t


def _moba_qkv_kernel(x_ref, g_ref, w_ref, cos_ref, sin_ref, q_ref, k_ref, vt_ref):
    d = D_MODEL
    h = _rms(x_ref[0], g_ref[...]).astype(BF16)
    cos_t = cos_ref[...]
    sin_t = sin_ref[...]
    q = jnp.dot(h, w_ref[:, 0:d], preferred_element_type=F32)
    k = jnp.dot(h, w_ref[:, d:2 * d], preferred_element_type=F32)
    v = jnp.dot(h, w_ref[:, 2 * d:3 * d], preferred_element_type=F32)
    for hd in range(N_HEADS):
        sl = slice(hd * HEAD_DIM, (hd + 1) * HEAD_DIM)
        q_ref[0, :, sl] = _rope_group(q[:, sl], cos_t, sin_t).astype(BF16)
        k_ref[0, :, sl] = _rope_group(k[:, sl], cos_t, sin_t).astype(BF16)
    vt_ref[0] = v.T.astype(BF16)


def _moba_qkv(x3d, gain, w_qkv, layer, cos_t, sin_t, *, tm=512):
    b, s, d = x3d.shape
    est = 2 * tm * d * 4 + d * 3 * d * 2 + 4 * tm * LANES * 4 + 6 * tm * d * 2 + 4 * tm * d * 4
    return pl.pallas_call(
        _moba_qkv_kernel,
        out_shape=(jax.ShapeDtypeStruct((b, s, d), BF16),
                   jax.ShapeDtypeStruct((b, s, d), BF16),
                   jax.ShapeDtypeStruct((b, d, s), BF16)),
        grid=(b, s // tm),
        in_specs=[
            pl.BlockSpec((1, tm, d), lambda i, j: (i, j, 0)),
            _resident((1, d), lambda i, j: (0, 0)),
            _resident((None, d, 3 * d), lambda i, j: (layer, 0, 0)),
            pl.BlockSpec((tm, LANES), lambda i, j: (j, 0)),
            pl.BlockSpec((tm, LANES), lambda i, j: (j, 0)),
        ],
        out_specs=(pl.BlockSpec((1, tm, d), lambda i, j: (i, j, 0)),
                   pl.BlockSpec((1, tm, d), lambda i, j: (i, j, 0)),
                   pl.BlockSpec((1, d, tm), lambda i, j: (i, 0, j))),
        compiler_params=pltpu.CompilerParams(
            dimension_semantics=("parallel", "parallel"), vmem_limit_bytes=_vmem_limit(est)),
        name="moba_qkv",
    )(x3d, gain.reshape(1, d), w_qkv, cos_t, sin_t)


def _mla_kv_kernel(x_ref, gin_ref, wd_ref, gkv_ref, wu_ref, cos_ref, sin_ref, ka_ref, vt_ref):
    h = _rms(x_ref[0], gin_ref[...]).astype(BF16)
    ckr = jnp.dot(h, wd_ref[...], preferred_element_type=F32)
    c_kv = _rms(ckr[:, :KV_LORA], gkv_ref[...]).astype(BF16)
    k_rope = _rope_group(ckr[:, KV_LORA:], cos_ref[...], sin_ref[...]).astype(BF16)
    kv = jnp.dot(c_kv, wu_ref[...], preferred_element_type=F32)
    for hd in range(N_HEADS):
        base = hd * MLA_QK_PAD
        ka_ref[0, :, base:base + HEAD_DIM] = kv[:, base:base + HEAD_DIM].astype(BF16)
        ka_ref[0, :, base + HEAD_DIM:base + MLA_QK_PAD] = k_rope
        vt_ref[0, hd * HEAD_DIM:(hd + 1) * HEAD_DIM, :] = (
            kv[:, base + HEAD_DIM:base + MLA_QK_PAD].T.astype(BF16))


def _mla_kv(x3d, kv_in_norm, wd_pad, kv_norm, w_ukv, cos_t, sin_t, *, tm=512):
    b, s, d = x3d.shape
    nd = wd_pad.shape[1]
    nu = w_ukv.shape[1]
    est = 2 * tm * d * 4 + d * nd * 2 + KV_LORA * nu * 2 + 2 * tm * nu * 2 + 2 * tm * d * 2 + 3 * tm * nu * 4
    return pl.pallas_call(
        _mla_kv_kernel,
        out_shape=(jax.ShapeDtypeStruct((b, s, N_HEADS * MLA_QK_PAD), BF16),
                   jax.ShapeDtypeStruct((b, d, s), BF16)),
        grid=(b, s // tm),
        in_specs=[
            pl.BlockSpec((1, tm, d), lambda i, j: (i, j, 0)),
            _resident((1, d), lambda i, j: (0, 0)),
            _resident((d, nd), lambda i, j: (0, 0)),
            _resident((1, KV_LORA), lambda i, j: (0, 0)),
            _resident((KV_LORA, nu), lambda i, j: (0, 0)),
            pl.BlockSpec((tm, LANES), lambda i, j: (j, 0)),
            pl.BlockSpec((tm, LANES), lambda i, j: (j, 0)),
        ],
        out_specs=(pl.BlockSpec((1, tm, N_HEADS * MLA_QK_PAD), lambda i, j: (i, j, 0)),
                   pl.BlockSpec((1, d, tm), lambda i, j: (i, 0, j))),
        compiler_params=pltpu.CompilerParams(
            dimension_semantics=("parallel", "parallel"), vmem_limit_bytes=_vmem_limit(est)),
        name="mla_shared_kv",
    )(x3d, kv_in_norm.reshape(1, d), wd_pad, kv_norm.reshape(1, KV_LORA), w_ukv, cos_t, sin_t)


def _mla_q_kernel(x_ref, g_ref, wd_ref, gq_ref, wu_ref, cos_ref, sin_ref, qa_ref):
    h = _rms(x_ref[0], g_ref[...]).astype(BF16)
    cq = jnp.dot(h, wd_ref[...], preferred_element_type=F32)
    cqn = _rms(cq, gq_ref[...]).astype(BF16)
    qa = jnp.dot(cqn, wu_ref[...], preferred_element_type=F32)
    cos_t = cos_ref[...]
    sin_t = sin_ref[...]
    for hd in range(N_HEADS):
        base = hd * MLA_QK_PAD
        qa_ref[0, :, base:base + HEAD_DIM] = qa[:, base:base + HEAD_DIM].astype(BF16)
        qa_ref[0, :, base + HEAD_DIM:base + MLA_QK_PAD] = _rope_group(
            qa[:, base + HEAD_DIM:base + MLA_QK_PAD], cos_t, sin_t).astype(BF16)


def _mla_q(x3d, gain, w_dq, q_norm, wu_pad, layer, cos_t, sin_t, *, tm=512):
    b, s, d = x3d.shape
    nu = wu_pad.shape[-1]
    est = 2 * tm * d * 4 + d * Q_LORA * 2 + Q_LORA * nu * 2 + 2 * tm * nu * 2 + 3 * tm * nu * 4
    return pl.pallas_call(
        _mla_q_kernel,
        out_shape=jax.ShapeDtypeStruct((b, s, nu), BF16),
        grid=(b, s // tm),
        in_specs=[
            pl.BlockSpec((1, tm, d), lambda i, j: (i, j, 0)),
            _resident((1, d), lambda i, j: (0, 0)),
            _resident((None, d, Q_LORA), lambda i, j: (layer, 0, 0)),
            _resident((1, Q_LORA), lambda i, j: (0, 0)),
            _resident((None, Q_LORA, nu), lambda i, j: (layer, 0, 0)),
            pl.BlockSpec((tm, LANES), lambda i, j: (j, 0)),
            pl.BlockSpec((tm, LANES), lambda i, j: (j, 0)),
        ],
        out_specs=pl.BlockSpec((1, tm, nu), lambda i, j: (i, j, 0)),
        compiler_params=pltpu.CompilerParams(
            dimension_semantics=("parallel", "parallel"), vmem_limit_bytes=_vmem_limit(est)),
        name="mla_q",
    )(x3d, gain.reshape(1, d), w_dq, q_norm.reshape(1, Q_LORA), wu_pad, cos_t, sin_t)


def _moba_keep(gate, own):
    rows = [gate[n:n + 1, :] for n in range(own)]
    keep = []
    for n in range(own):
        rank = jnp.zeros_like(rows[n])
        for m in range(own):
            if m < n:
                rank = rank + jnp.where(rows[m] >= rows[n], 1.0, 0.0)
            elif m > n:
                rank = rank + jnp.where(rows[m] > rows[n], 1.0, 0.0)
        keep.append(rank < float(MOBA_TOPK))
    return keep


def _attn_kernel(q_ref, k_ref, vt_ref, o_ref, *, moba, exp_scale):
    s_len = q_ref.shape[1]
    blk = MOBA_BLOCK
    nb = s_len // blk
    nt = (((1,), (1,)), ((), ()))
    key_i = lax.broadcasted_iota(jnp.int32, (blk, blk), 0)
    qry_i = lax.broadcasted_iota(jnp.int32, (blk, blk), 1)
    causal_bias = jnp.where(key_i <= qry_i, 0.0, -jnp.inf)

    if moba:
        sub_i = lax.broadcasted_iota(jnp.int32, (8, HEAD_DIM), 0)
        km = jnp.zeros((8, HEAD_DIM), F32)
        for n in range(nb - 1):
            row = jnp.sum(k_ref[0, n * blk:(n + 1) * blk, :].astype(F32), axis=0, keepdims=True)
            km = jnp.where(sub_i == n, row * (1.0 / blk), km)
        km_hi = km.astype(BF16)
        km_lo = (km - km_hi.astype(F32)).astype(BF16)
        km_hl = jnp.concatenate([km_hi, km_lo], axis=0)

    dv = vt_ref.shape[1]
    ones_rows = jnp.ones((SUM_ROWS, s_len), BF16)

    def keep_rows(qi, q_blk):
        if not (moba and qi > MOBA_TOPK):
            return None
        g2 = lax.dot_general(km_hl, q_blk, nt, preferred_element_type=F32)
        return _moba_keep(g2[0:8, :] + g2[8:16, :], qi)

    def values(qi, p_list):
        p_t = p_list[0] if len(p_list) == 1 else jnp.concatenate(p_list, axis=0)
        kv = (qi + 1) * blk
        vt_aug = jnp.concatenate([vt_ref[0, :, 0:kv], ones_rows[:, 0:kv]], axis=0)
        out_aug = jnp.dot(vt_aug, p_t, preferred_element_type=F32)
        denom = out_aug[dv:dv + 1, :]
        out_t = out_aug[0:dv, :] * (1.0 / denom)
        o_ref[0, qi * blk:(qi + 1) * blk, :] = out_t.astype(BF16).T
        return denom

    lo, hi = 2.0 ** -DENOM_LOG2_RANGE, 2.0 ** DENOM_LOG2_RANGE
    suspect = None
    for qi in reversed(range(nb)):
        q_blk = q_ref[0, qi * blk:(qi + 1) * blk, :]
        keep = keep_rows(qi, q_blk)
        p_list = []
        for n in range(qi + 1):
            k_blk = k_ref[0, n * blk:(n + 1) * blk, :]
            t = lax.dot_general(k_blk, q_blk, nt, preferred_element_type=F32) * exp_scale
            if n == qi:
                t = t + causal_bias
            p = jnp.exp2(t).astype(BF16)
            if n < qi and keep is not None:
                p = p * jnp.where(keep[n], 1.0, 0.0).astype(BF16)
            p_list.append(p)
        denom = values(qi, p_list)
        flag = jnp.where(jnp.logical_and(denom >= lo, denom <= hi), 0.0, 1.0)
        suspect = flag if suspect is None else jnp.maximum(suspect, flag)

    @pl.when(jnp.max(suspect) > 0.0)
    def _():
        _shifted_softmax_pass(q_ref, k_ref, keep_rows, values, causal_bias, nb, blk, exp_scale)


def _shifted_softmax_pass(q_ref, k_ref, keep_rows, values, causal_bias, nb, blk, exp_scale):
    nt = (((1,), (1,)), ((), ()))

    def logits_begin(qi):
        q_blk = q_ref[0, qi * blk:(qi + 1) * blk, :]
        keep = keep_rows(qi, q_blk)
        sel_bias = None if keep is None else [jnp.where(kp, 0.0, -jnp.inf) for kp in keep]
        return dict(qi=qi, q=q_blk, bias=sel_bias, s=[], m8=None)

    def logits_step(st, n):
        qi = st["qi"]
        k_blk = k_ref[0, n * blk:(n + 1) * blk, :]
        s = lax.dot_general(k_blk, st["q"], nt, preferred_element_type=F32)
        if n == qi:
            s = s + causal_bias
        st["s"].append(s)
        bm8 = jnp.max(s.reshape(blk // 8, 8, blk), axis=0)
        if n < qi and st["bias"] is not None:
            bm8 = bm8 + st["bias"][n]
        st["m8"] = bm8 if st["m8"] is None else jnp.maximum(st["m8"], bm8)

    def weights_step(st, m, n):
        m_n = m - st["bias"][n] if (n < st["qi"] and st["bias"] is not None) else m
        return jnp.exp2((st["s"][n] - m_n) * exp_scale).astype(BF16)

    def logits_all(qi):
        st = logits_begin(qi)
        for n in range(qi + 1):
            logits_step(st, n)
        return st

    ahead = [logits_all(qi) for qi in range(min(ATTN_LOOKAHEAD, nb))]
    for qi in range(nb):
        if qi + ATTN_LOOKAHEAD < nb:
            ahead.append(logits_all(qi + ATTN_LOOKAHEAD))
        cur = ahead.pop(0)
        m = jnp.max(cur["m8"], axis=0, keepdims=True)
        values(qi, [weights_step(cur, m, n) for n in range(qi + 1)])


def _attention(q, k, vt, *, moba, qk_dim, scale):
    b, s, _ = q.shape
    dv = HEAD_DIM
    est = 2 * (2 * s * qk_dim * 2 + 2 * s * dv * 2) + 24 * MOBA_BLOCK * s * 4
    return pl.pallas_call(
        functools.partial(_attn_kernel, moba=moba, exp_scale=scale * LOG2_E),
        out_shape=jax.ShapeDtypeStruct((b, s, N_HEADS * dv), BF16),
        grid=(b, N_HEADS),
        in_specs=[
            pl.BlockSpec((1, s, qk_dim), lambda i, j: (i, 0, j)),
            pl.BlockSpec((1, s, qk_dim), lambda i, j: (i, 0, j)),
            pl.BlockSpec((1, dv, s), lambda i, j: (i, j, 0)),
        ],
        out_specs=pl.BlockSpec((1, s, dv), lambda i, j: (i, 0, j)),
        compiler_params=pltpu.CompilerParams(
            dimension_semantics=("parallel", "parallel"), vmem_limit_bytes=_vmem_limit(est)),
        name="moba_attn" if moba else "mla_attn",
    )(q, k, vt)


def _oproj_kernel(x_ref, o_ref, w_ref, g_ref, y_ref):
    mix = jnp.dot(o_ref[...], w_ref[...], preferred_element_type=F32)
    y_ref[...] = x_ref[...] + _rms(mix, g_ref[...])


def _oproj(x2d, o2d, w_o, layer, gain, *, tm=512):
    mt, d = x2d.shape
    est = 4 * tm * d * 4 + 2 * tm * d * 2 + d * d * 2 + 2 * tm * d * 4
    return pl.pallas_call(
        _oproj_kernel,
        out_shape=jax.ShapeDtypeStruct((mt, d), F32),
        grid=(mt // tm,),
        in_specs=[
            pl.BlockSpec((tm, d), lambda i: (i, 0)),
            pl.BlockSpec((tm, d), lambda i: (i, 0)),
            _resident((None, d, d), lambda i: (layer, 0, 0)),
            _resident((1, d), lambda i: (0, 0)),
        ],
        out_specs=pl.BlockSpec((tm, d), lambda i: (i, 0)),
        compiler_params=pltpu.CompilerParams(
            dimension_semantics=("parallel",), vmem_limit_bytes=_vmem_limit(est)),
        name="mixer_oproj",
    )(x2d, o2d, w_o, gain.reshape(1, d))


def _rope_tables(seq_len, dim):
    inv_freq = 1.0 / (ROPE_THETA ** (jnp.arange(0, dim, 2, dtype=F32) / dim))
    ang = jnp.arange(seq_len, dtype=F32)[:, None] * inv_freq[None, :]
    return jnp.cos(ang), jnp.sin(ang)


def _rope_group_layout(main, rope):
    half = ROPE_DIM // 2
    zeros = jnp.zeros(rope.shape[:-1] + (LANES // 2 - half,), rope.dtype)
    return jnp.concatenate([main, rope[..., :half], zeros, rope[..., half:], zeros], axis=-1)


def kernel(x, norm_gains, ffn_w1, ffn_w3, ffn_w2, moba_w_qkv, moba_w_o, mla_w_dq, mla_q_norm,
           mla_w_uq, mla_w_o, kv_in_norm, w_dkv_kr, kv_norm, w_ukv):
    b, s, d = x.shape
    depth = norm_gains.shape[0]
    n_a = moba_w_qkv.shape[0]

    cos_a, sin_a = _rope_tables(s, HEAD_DIM)
    cos_moba = jnp.concatenate([cos_a, cos_a], axis=1)
    sin_moba = jnp.concatenate([-sin_a, sin_a], axis=1)
    cos_r, sin_r = _rope_tables(s, ROPE_DIM)
    zpad = jnp.zeros_like(cos_r)
    cos_mla = jnp.concatenate([cos_r, zpad, cos_r, zpad], axis=1)
    sin_mla = jnp.concatenate([-sin_r, zpad, sin_r, zpad], axis=1)

    w1 = ffn_w1.astype(BF16)
    w3 = ffn_w3.astype(BF16)
    w2 = ffn_w2.astype(BF16)
    w_qkv = moba_w_qkv.astype(BF16)
    w_o_a = moba_w_o.astype(BF16)
    w_o_b = mla_w_o.astype(BF16)
    w_dq = mla_w_dq.astype(BF16)
    wd_pad = _rope_group_layout(w_dkv_kr[:, :KV_LORA], w_dkv_kr[:, KV_LORA:]).astype(BF16)
    uq = mla_w_uq.reshape(mla_w_uq.shape[0], Q_LORA, N_HEADS, HEAD_DIM + ROPE_DIM)
    wu_pad = _rope_group_layout(uq[..., :HEAD_DIM], uq[..., HEAD_DIM:])
    wu_pad = wu_pad.reshape(mla_w_uq.shape[0], Q_LORA, N_HEADS * MLA_QK_PAD).astype(BF16)
    w_ukv_b = w_ukv.astype(BF16)

    x2 = x.reshape(b * s, d)
    k_aug = vt_shared = None
    for l in range(depth):
        g = norm_gains[l]
        x2 = _ffn(x2, norm_gains, w1, w3, w2, l, 0)
        x3 = x2.reshape(b, s, d)
        if l < n_a:
            q, k, vt = _moba_qkv(x3, g[2], w_qkv, l, cos_moba, sin_moba)
            o = _attention(q, k, vt, moba=True, qk_dim=HEAD_DIM, scale=HEAD_DIM ** -0.5)
            x2 = _oproj(x2, o.reshape(b * s, d), w_o_a, l, g[3])
        else:
            j = l - n_a
            q_aug = _mla_q(x3, g[2], w_dq, mla_q_norm[j], wu_pad, j, cos_mla, sin_mla)
            o = _attention(q_aug, k_aug, vt_shared, moba=False, qk_dim=MLA_QK_PAD,
                           scale=(HEAD_DIM + ROPE_DIM) ** -0.5)
            x2 = _oproj(x2, o.reshape(b * s, d), w_o_b, j, g[3])
        x2 = _ffn(x2, norm_gains, w1, w3, w2, l, 1)
        if l == n_a - 1:
            k_aug, vt_shared = _mla_kv(x2.reshape(b, s, d), kv_in_norm, wd_pad, kv_norm,
                                       w_ukv_b, cos_mla, sin_mla)
    return x2.reshape(b, s, d)
```

```python
import functools

import jax
import jax.numpy as jnp
from jax import lax
from jax.experimental import pallas as pl
from jax.experimental.pallas import tpu as pltpu

D_MODEL = 1024
N_HEADS = 8
HEAD_DIM = 128
MOBA_BLOCK = 256
MOBA_TOPK = 3
ROPE_DIM = 64
KV_LORA = 256
Q_LORA = 512
ROPE_THETA = 10000.0
NORM_EPS = 1e-6
LANES = 128
MLA_QK_PAD = 2 * LANES
SUM_ROWS = 16
ATTN_LOOKAHEAD = 3
DENOM_LOG2_RANGE = 60.0
V7X_VMEM_BYTES = 64 * 1024 * 1024
LOG2_E = 1.4426950408889634

F32 = jnp.float32
BF16 = jnp.bfloat16


def _vmem_limit(estimate_bytes):
    return int(min(estimate_bytes * 3 // 2, V7X_VMEM_BYTES * 15 // 16))


def _rms(x, g):
    ms = jnp.mean(x * x, axis=-1, keepdims=True)
    return x * lax.rsqrt(ms + NORM_EPS) * g


def _resident(shape, index_map):
    return pl.BlockSpec(shape, index_map, pipeline_mode=pl.Buffered(1))


def _ffn_kernel(*refs, ff_chunks, mixer):
    if mixer:
        x_ref, a_ref, wo_ref, g_ref, w1_ref, w3_ref, w2_ref, o_ref, s_ref = refs
        mix = jnp.dot(a_ref[...], wo_ref[...], preferred_element_type=F32)
        x = x_ref[...] + _rms(mix, g_ref[0:1, :])
        pre, post = 1, 2
    else:
        x_ref, g_ref, w1_ref, w3_ref, w2_ref, o_ref, s_ref = refs
        x = x_ref[...]
        pre, post = 0, 1
    h = _rms(x, g_ref[pre:pre + 1, :]).astype(BF16)
    for start, size in ff_chunks:
        a = jnp.dot(h, w1_ref[:, start:start + size], preferred_element_type=F32)
        b = jnp.dot(h, w3_ref[:, start:start + size], preferred_element_type=F32)
        silu = a * (1.0 / (1.0 + jnp.exp(-a)))
        s_ref[:, start:start + size] = (silu * b).astype(BF16)
    y = jnp.dot(s_ref[...], w2_ref[...], preferred_element_type=F32)
    o_ref[...] = x + 0.5 * _rms(y, g_ref[post:post + 1, :])


def _ffn(x2d, gains, w1, w3, w2, layer, half, mixer=None, *, tm=1024):
    mt, d = x2d.shape
    ff = w1.shape[-1]
    chunk = 4 * LANES
    ff_chunks = tuple((s, min(chunk, ff - s)) for s in range(0, ff, chunk))
    est = (4 * tm * d * 4 + 3 * d * ff * 2 + tm * ff * 2 + 4 * tm * chunk * 4 + 2 * tm * d * 4)
    row = pl.BlockSpec((tm, d), lambda i: (i, 0))
    gain_spec = _resident((None, None, 3, d), lambda i: (layer, half, 0, 0))
    weight_specs = [
        _resident((None, None, d, ff), lambda i: (layer, half, 0, 0)),
        _resident((None, None, d, ff), lambda i: (layer, half, 0, 0)),
        _resident((None, None, ff, d), lambda i: (layer, half, 0, 0)),
    ]
    gains3 = gains.reshape(gains.shape[0], 2, 3, d)
    if mixer is None:
        in_specs = [row, gain_spec] + weight_specs
        operands = (x2d, gains3, w1, w3, w2)
    else:
        attn_out, w_o, w_o_layer = mixer
        est += 2 * tm * d * 2 + d * d * 2 + tm * d * 4
        in_specs = [row, row, _resident((None, d, d), lambda i: (w_o_layer, 0, 0)), gain_spec] + weight_specs
        operands = (x2d, attn_out, w_o, gains3, w1, w3, w2)
    return pl.pallas_call(
        functools.partial(_ffn_kernel, ff_chunks=ff_chunks, mixer=mixer is not None),
        out_shape=jax.ShapeDtypeStruct((mt, d), F32),
        grid=(mt // tm,),
        in_specs=in_specs,
        out_specs=row,
        scratch_shapes=[pltpu.VMEM((tm, ff), BF16)],
        compiler_params=pltpu.CompilerParams(
            dimension_semantics=("parallel",), vmem_limit_bytes=_vmem_limit(est)),
        name=f"ffn_l{layer}_h{half}",
    )(*operands)


def _rope_group(x, cos_t, sin_t):
    return x * cos_t + pltpu.roll(x, LANES // 2, 1) * sin_t


def _moba_qkv_kernel(x_ref, g_ref, w_ref, cos_ref, sin_ref, q_ref, k_ref, vt_ref):
    d = D_MODEL
    h = _rms(x_ref[0], g_ref[...]).astype(BF16)
    cos_t = cos_ref[...]
    sin_t = sin_ref[...]
    q = jnp.dot(h, w_ref[:, 0:d], preferred_element_type=F32)
    k = jnp.dot(h, w_ref[:, d:2 * d], preferred_element_type=F32)
    v = jnp.dot(h, w_ref[:, 2 * d:3 * d], preferred_element_type=F32)
    for hd in range(N_HEADS):
        sl = slice(hd * HEAD_DIM, (hd + 1) * HEAD_DIM)
        q_ref[0, :, sl] = _rope_group(q[:, sl], cos_t, sin_t).astype(BF16)
        k_ref[0, :, sl] = _rope_group(k[:, sl], cos_t, sin_t).astype(BF16)
    vt_ref[0] = v.T.astype(BF16)


def _moba_qkv(x3d, gain, w_qkv, layer, cos_t, sin_t, *, tm=1024):
    b, s, d = x3d.shape
    est = 2 * tm * d * 4 + d * 3 * d * 2 + 4 * tm * LANES * 4 + 6 * tm * d * 2 + 4 * tm * d * 4
    return pl.pallas_call(
        _moba_qkv_kernel,
        out_shape=(jax.ShapeDtypeStruct((b, s, d), BF16),
                   jax.ShapeDtypeStruct((b, s, d), BF16),
                   jax.ShapeDtypeStruct((b, d, s), BF16)),
        grid=(b, s // tm),
        in_specs=[
            pl.BlockSpec((1, tm, d), lambda i, j: (i, j, 0)),
            _resident((1, d), lambda i, j: (0, 0)),
            _resident((None, d, 3 * d), lambda i, j: (layer, 0, 0)),
            pl.BlockSpec((tm, LANES), lambda i, j: (j, 0)),
            pl.BlockSpec((tm, LANES), lambda i, j: (j, 0)),
        ],
        out_specs=(pl.BlockSpec((1, tm, d), lambda i, j: (i, j, 0)),
                   pl.BlockSpec((1, tm, d), lambda i, j: (i, j, 0)),
                   pl.BlockSpec((1, d, tm), lambda i, j: (i, 0, j))),
        compiler_params=pltpu.CompilerParams(
            dimension_semantics=("parallel", "parallel"), vmem_limit_bytes=_vmem_limit(est)),
        name="moba_qkv",
    )(x3d, gain.reshape(1, d), w_qkv, cos_t, sin_t)


def _mla_kv_kernel(x_ref, gin_ref, wd_ref, gkv_ref, wu_ref, cos_ref, sin_ref, ka_ref, vt_ref):
    h = _rms(x_ref[0], gin_ref[...]).astype(BF16)
    ckr = jnp.dot(h, wd_ref[...], preferred_element_type=F32)
    c_kv = _rms(ckr[:, :KV_LORA], gkv_ref[...]).astype(BF16)
    k_rope = _rope_group(ckr[:, KV_LORA:], cos_ref[...], sin_ref[...]).astype(BF16)
    kv = jnp.dot(c_kv, wu_ref[...], preferred_element_type=F32)
    for hd in range(N_HEADS):
        base = hd * MLA_QK_PAD
        ka_ref[0, :, base:base + HEAD_DIM] = kv[:, base:base + HEAD_DIM].astype(BF16)
        ka_ref[0, :, base + HEAD_DIM:base + MLA_QK_PAD] = k_rope
        vt_ref[0, hd * HEAD_DIM:(hd + 1) * HEAD_DIM, :] = (
            kv[:, base + HEAD_DIM:base + MLA_QK_PAD].T.astype(BF16))


def _mla_kv(x3d, kv_in_norm, wd_pad, kv_norm, w_ukv, cos_t, sin_t, *, tm=1024):
    b, s, d = x3d.shape
    nd = wd_pad.shape[1]
    nu = w_ukv.shape[1]
    est = 2 * tm * d * 4 + d * nd * 2 + KV_LORA * nu * 2 + 2 * tm * nu * 2 + 2 * tm * d * 2 + 3 * tm * nu * 4
    return pl.pallas_call(
        _mla_kv_kernel,
        out_shape=(jax.ShapeDtypeStruct((b, s, N_HEADS * MLA_QK_PAD), BF16),
                   jax.ShapeDtypeStruct((b, d, s), BF16)),
        grid=(b, s // tm),
        in_specs=[
            pl.BlockSpec((1, tm, d), lambda i, j: (i, j, 0)),
            _resident((1, d), lambda i, j: (0, 0)),
            _resident((d, nd), lambda i, j: (0, 0)),
            _resident((1, KV_LORA), lambda i, j: (0, 0)),
            _resident((KV_LORA, nu), lambda i, j: (0, 0)),
            pl.BlockSpec((tm, LANES), lambda i, j: (j, 0)),
            pl.BlockSpec((tm, LANES), lambda i, j: (j, 0)),
        ],
        out_specs=(pl.BlockSpec((1, tm, N_HEADS * MLA_QK_PAD), lambda i, j: (i, j, 0)),
                   pl.BlockSpec((1, d, tm), lambda i, j: (i, 0, j))),
        compiler_params=pltpu.CompilerParams(
            dimension_semantics=("parallel", "parallel"), vmem_limit_bytes=_vmem_limit(est)),
        name="mla_shared_kv",
    )(x3d, kv_in_norm.reshape(1, d), wd_pad, kv_norm.reshape(1, KV_LORA), w_ukv, cos_t, sin_t)


def _mla_q_kernel(x_ref, g_ref, wd_ref, gq_ref, wu_ref, cos_ref, sin_ref, qa_ref):
    h = _rms(x_ref[0], g_ref[...]).astype(BF16)
    cq = jnp.dot(h, wd_ref[...], preferred_element_type=F32)
    cqn = _rms(cq, gq_ref[...]).astype(BF16)
    qa = jnp.dot(cqn, wu_ref[...], preferred_element_type=F32)
    cos_t = cos_ref[...]
    sin_t = sin_ref[...]
    for hd in range(N_HEADS):
        base = hd * MLA_QK_PAD
        qa_ref[0, :, base:base + HEAD_DIM] = qa[:, base:base + HEAD_DIM].astype(BF16)
        qa_ref[0, :, base + HEAD_DIM:base + MLA_QK_PAD] = _rope_group(
            qa[:, base + HEAD_DIM:base + MLA_QK_PAD], cos_t, sin_t).astype(BF16)


def _mla_q(x3d, gain, w_dq, q_norm, wu_pad, layer, cos_t, sin_t, *, tm=1024):
    b, s, d = x3d.shape
    nu = wu_pad.shape[-1]
    est = 2 * tm * d * 4 + d * Q_LORA * 2 + Q_LORA * nu * 2 + 2 * tm * nu * 2 + 3 * tm * nu * 4
    return pl.pallas_call(
        _mla_q_kernel,
        out_shape=jax.ShapeDtypeStruct((b, s, nu), BF16),
        grid=(b, s // tm),
        in_specs=[
            pl.BlockSpec((1, tm, d), lambda i, j: (i, j, 0)),
            _resident((1, d), lambda i, j: (0, 0)),
            _resident((None, d, Q_LORA), lambda i, j: (layer, 0, 0)),
            _resident((1, Q_LORA), lambda i, j: (0, 0)),
            _resident((None, Q_LORA, nu), lambda i, j: (layer, 0, 0)),
            pl.BlockSpec((tm, LANES), lambda i, j: (j, 0)),
            pl.BlockSpec((tm, LANES), lambda i, j: (j, 0)),
        ],
        out_specs=pl.BlockSpec((1, tm, nu), lambda i, j: (i, j, 0)),
        compiler_params=pltpu.CompilerParams(
            dimension_semantics=("parallel", "parallel"), vmem_limit_bytes=_vmem_limit(est)),
        name="mla_q",
    )(x3d, gain.reshape(1, d), w_dq, q_norm.reshape(1, Q_LORA), wu_pad, cos_t, sin_t)


def _moba_keep(gate, own):
    rows = [gate[n:n + 1, :] for n in range(own)]
    keep = []
    for n in range(own):
        rank = jnp.zeros_like(rows[n])
        for m in range(own):
            if m < n:
                rank = rank + jnp.where(rows[m] >= rows[n], 1.0, 0.0)
            elif m > n:
                rank = rank + jnp.where(rows[m] > rows[n], 1.0, 0.0)
        keep.append(rank < float(MOBA_TOPK))
    return keep


def _attn_kernel(q_ref, k_ref, vt_ref, o_ref, *, moba, exp_scale):
    s_len = q_ref.shape[1]
    blk = MOBA_BLOCK
    nb = s_len // blk
    nt = (((1,), (1,)), ((), ()))
    key_i = lax.broadcasted_iota(jnp.int32, (blk, blk), 0)
    qry_i = lax.broadcasted_iota(jnp.int32, (blk, blk), 1)
    causal_bias = jnp.where(key_i <= qry_i, 0.0, -jnp.inf)

    if moba:
        sub_i = lax.broadcasted_iota(jnp.int32, (8, HEAD_DIM), 0)
        km = jnp.zeros((8, HEAD_DIM), F32)
        for n in range(nb - 1):
            row = jnp.sum(k_ref[0, n * blk:(n + 1) * blk, :].astype(F32), axis=0, keepdims=True)
            km = jnp.where(sub_i == n, row * (1.0 / blk), km)
        km_hi = km.astype(BF16)
        km_lo = (km - km_hi.astype(F32)).astype(BF16)
        km_hl = jnp.concatenate([km_hi, km_lo], axis=0)

    dv = vt_ref.shape[1]
    ones_rows = jnp.ones((SUM_ROWS, s_len), BF16)

    def keep_rows(qi, q_blk):
        if not (moba and qi > MOBA_TOPK):
            return None
        g2 = lax.dot_general(km_hl, q_blk, nt, preferred_element_type=F32)
        return _moba_keep(g2[0:8, :] + g2[8:16, :], qi)

    def values(qi, p_list):
        p_t = p_list[0] if len(p_list) == 1 else jnp.concatenate(p_list, axis=0)
        kv = (qi + 1) * blk
        vt_aug = jnp.concatenate([vt_ref[0, :, 0:kv], ones_rows[:, 0:kv]], axis=0)
        out_aug = jnp.dot(vt_aug, p_t, preferred_element_type=F32)
        denom = out_aug[dv:dv + 1, :]
        out_t = out_aug[0:dv, :] * (1.0 / denom)
        o_ref[0, qi * blk:(qi + 1) * blk, :] = out_t.astype(BF16).T
        return denom

    lo, hi = 2.0 ** -DENOM_LOG2_RANGE, 2.0 ** DENOM_LOG2_RANGE
    suspect = None
    for qi in reversed(range(nb)):
        q_blk = q_ref[0, qi * blk:(qi + 1) * blk, :]
        keep = keep_rows(qi, q_blk)
        p_list = []
        for n in range(qi + 1):
            k_blk = k_ref[0, n * blk:(n + 1) * blk, :]
            t = lax.dot_general(k_blk, q_blk, nt, preferred_element_type=F32) * exp_scale
            if n == qi:
                t = t + causal_bias
            p = jnp.exp2(t).astype(BF16)
            if n < qi and keep is not None:
                p = p * jnp.where(keep[n], 1.0, 0.0).astype(BF16)
            p_list.append(p)
        denom = values(qi, p_list)
        flag = jnp.where(jnp.logical_and(denom >= lo, denom <= hi), 0.0, 1.0)
        suspect = flag if suspect is None else jnp.maximum(suspect, flag)

    @pl.when(jnp.max(suspect) > 0.0)
    def _():
        _shifted_softmax_pass(q_ref, k_ref, keep_rows, values, causal_bias, nb, blk, exp_scale)


def _shifted_softmax_pass(q_ref, k_ref, keep_rows, values, causal_bias, nb, blk, exp_scale):
    nt = (((1,), (1,)), ((), ()))

    def logits_begin(qi):
        q_blk = q_ref[0, qi * blk:(qi + 1) * blk, :]
        keep = keep_rows(qi, q_blk)
        sel_bias = None if keep is None else [jnp.where(kp, 0.0, -jnp.inf) for kp in keep]
        return dict(qi=qi, q=q_blk, bias=sel_bias, s=[], m8=None)

    def logits_step(st, n):
        qi = st["qi"]
        k_blk = k_ref[0, n * blk:(n + 1) * blk, :]
        s = lax.dot_general(k_blk, st["q"], nt, preferred_element_type=F32)
        if n == qi:
            s = s + causal_bias
        st["s"].append(s)
        bm8 = jnp.max(s.reshape(blk // 8, 8, blk), axis=0)
        if n < qi and st["bias"] is not None:
            bm8 = bm8 + st["bias"][n]
        st["m8"] = bm8 if st["m8"] is None else jnp.maximum(st["m8"], bm8)

    def weights_step(st, m, n):
        m_n = m - st["bias"][n] if (n < st["qi"] and st["bias"] is not None) else m
        return jnp.exp2((st["s"][n] - m_n) * exp_scale).astype(BF16)

    def logits_all(qi):
        st = logits_begin(qi)
        for n in range(qi + 1):
            logits_step(st, n)
        return st

    ahead = [logits_all(qi) for qi in range(min(ATTN_LOOKAHEAD, nb))]
    for qi in range(nb):
        if qi + ATTN_LOOKAHEAD < nb:
            ahead.append(logits_all(qi + ATTN_LOOKAHEAD))
        cur = ahead.pop(0)
        m = jnp.max(cur["m8"], axis=0, keepdims=True)
        values(qi, [weights_step(cur, m, n) for n in range(qi + 1)])


def _attention(q, k, vt, *, moba, qk_dim, scale):
    b, s, _ = q.shape
    dv = HEAD_DIM
    est = 2 * (2 * s * qk_dim * 2 + 2 * s * dv * 2) + 24 * MOBA_BLOCK * s * 4
    return pl.pallas_call(
        functools.partial(_attn_kernel, moba=moba, exp_scale=scale * LOG2_E),
        out_shape=jax.ShapeDtypeStruct((b, s, N_HEADS * dv), BF16),
        grid=(b, N_HEADS),
        in_specs=[
            pl.BlockSpec((1, s, qk_dim), lambda i, j: (i, 0, j)),
            pl.BlockSpec((1, s, qk_dim), lambda i, j: (i, 0, j)),
            pl.BlockSpec((1, dv, s), lambda i, j: (i, j, 0)),
        ],
        out_specs=pl.BlockSpec((1, s, dv), lambda i, j: (i, 0, j)),
        compiler_params=pltpu.CompilerParams(
            dimension_semantics=("parallel", "parallel"), vmem_limit_bytes=_vmem_limit(est)),
        name="moba_attn" if moba else "mla_attn",
    )(q, k, vt)


def _rope_tables(seq_len, dim):
    inv_freq = 1.0 / (ROPE_THETA ** (jnp.arange(0, dim, 2, dtype=F32) / dim))
    ang = jnp.arange(seq_len, dtype=F32)[:, None] * inv_freq[None, :]
    return jnp.cos(ang), jnp.sin(ang)


def _rope_group_layout(main, rope):
    half = ROPE_DIM // 2
    zeros = jnp.zeros(rope.shape[:-1] + (LANES // 2 - half,), rope.dtype)
    return jnp.concatenate([main, rope[..., :half], zeros, rope[..., half:], zeros], axis=-1)


def kernel(x, norm_gains, ffn_w1, ffn_w3, ffn_w2, moba_w_qkv, moba_w_o, mla_w_dq, mla_q_norm,
           mla_w_uq, mla_w_o, kv_in_norm, w_dkv_kr, kv_norm, w_ukv):
    b, s, d = x.shape
    depth = norm_gains.shape[0]
    n_a = moba_w_qkv.shape[0]

    cos_a, sin_a = _rope_tables(s, HEAD_DIM)
    cos_moba = jnp.concatenate([cos_a, cos_a], axis=1)
    sin_moba = jnp.concatenate([-sin_a, sin_a], axis=1)
    cos_r, sin_r = _rope_tables(s, ROPE_DIM)
    zpad = jnp.zeros_like(cos_r)
    cos_mla = jnp.concatenate([cos_r, zpad, cos_r, zpad], axis=1)
    sin_mla = jnp.concatenate([-sin_r, zpad, sin_r, zpad], axis=1)

    w1 = ffn_w1.astype(BF16)
    w3 = ffn_w3.astype(BF16)
    w2 = ffn_w2.astype(BF16)
    w_qkv = moba_w_qkv.astype(BF16)
    w_o_a = moba_w_o.astype(BF16)
    w_o_b = mla_w_o.astype(BF16)
    w_dq = mla_w_dq.astype(BF16)
    wd_pad = _rope_group_layout(w_dkv_kr[:, :KV_LORA], w_dkv_kr[:, KV_LORA:]).astype(BF16)
    uq = mla_w_uq.reshape(mla_w_uq.shape[0], Q_LORA, N_HEADS, HEAD_DIM + ROPE_DIM)
    wu_pad = _rope_group_layout(uq[..., :HEAD_DIM], uq[..., HEAD_DIM:])
    wu_pad = wu_pad.reshape(mla_w_uq.shape[0], Q_LORA, N_HEADS * MLA_QK_PAD).astype(BF16)
    w_ukv_b = w_ukv.astype(BF16)

    x2 = x.reshape(b * s, d)
    k_aug = vt_shared = None
    for l in range(depth):
        g = norm_gains[l]
        x2 = _ffn(x2, norm_gains, w1, w3, w2, l, 0)
        x3 = x2.reshape(b, s, d)
        if l < n_a:
            q, k, vt = _moba_qkv(x3, g[2], w_qkv, l, cos_moba, sin_moba)
            o = _attention(q, k, vt, moba=True, qk_dim=HEAD_DIM, scale=HEAD_DIM ** -0.5)
            mixer = (o.reshape(b * s, d), w_o_a, l)
        else:
            j = l - n_a
            q_aug = _mla_q(x3, g[2], w_dq, mla_q_norm[j], wu_pad, j, cos_mla, sin_mla)
            o = _attention(q_aug, k_aug, vt_shared, moba=False, qk_dim=MLA_QK_PAD,
                           scale=(HEAD_DIM + ROPE_DIM) ** -0.5)
            mixer = (o.reshape(b * s, d), w_o_b, j)
        x2 = _ffn(x2, norm_gains, w1, w3, w2, l, 1, mixer)
        if l == n_a - 1:
            k_aug, vt_shared = _mla_kv(x2.reshape(b, s, d), kv_in_norm, wd_pad, kv_norm,
                                       w_ukv_b, cos_mla, sin_mla)
    return x2.reshape(b, s, d)
```

```python
import functools

import jax
import jax.numpy as jnp
from jax import lax
from jax.experimental import pallas as pl
from jax.experimental.pallas import tpu as pltpu

D_MODEL = 1024
N_HEADS = 8
HEAD_DIM = 128
MOBA_BLOCK = 256
MOBA_TOPK = 3
ROPE_DIM = 64
KV_LORA = 256
Q_LORA = 512
ROPE_THETA = 10000.0
NORM_EPS = 1e-6
LANES = 128
MLA_QK_PAD = 2 * LANES
SUM_ROWS = 16
FFN_ROW_GROUPS = 2
ATTN_HEADS_PER_STEP = 4
ATTN_LOOKAHEAD = 3
DENOM_LOG2_RANGE = 60.0
V7X_VMEM_BYTES = 64 * 1024 * 1024
LOG2_E = 1.4426950408889634

F32 = jnp.float32
BF16 = jnp.bfloat16


def _vmem_limit(estimate_bytes):
    return int(min(estimate_bytes * 3 // 2, V7X_VMEM_BYTES * 15 // 16))


def _rms(x, g):
    ms = jnp.mean(x * x, axis=-1, keepdims=True)
    return x * lax.rsqrt(ms + NORM_EPS) * g


def _resident(shape, index_map):
    return pl.BlockSpec(shape, index_map, pipeline_mode=pl.Buffered(1))


def _ffn_kernel(*refs, ff_chunks, mixer):
    if mixer:
        x_ref, a_ref, wo_ref, g_ref, w1_ref, w3_ref, w2_ref, o_ref, s_ref = refs
        pre, post = 1, 2
    else:
        x_ref, g_ref, w1_ref, w3_ref, w2_ref, o_ref, s_ref = refs
        pre, post = 0, 1
    tm = x_ref.shape[0]
    groups = [slice(r, r + tm // FFN_ROW_GROUPS) for r in range(0, tm, tm // FFN_ROW_GROUPS)]
    xs, hs = [], []
    for rows in groups:
        x = x_ref[rows, :]
        if mixer:
            mix = lax.dot_general(a_ref[:, rows], wo_ref[...], (((0,), (0,)), ((), ())),
                                  preferred_element_type=F32)
            x = x + _rms(mix, g_ref[0:1, :])
        xs.append(x)
        hs.append(_rms(x, g_ref[pre:pre + 1, :]).astype(BF16))
    for rows, h in zip(groups, hs):
        for start, size in ff_chunks:
            a = jnp.dot(h, w1_ref[:, start:start + size], preferred_element_type=F32)
            b = jnp.dot(h, w3_ref[:, start:start + size], preferred_element_type=F32)
            silu = a * (1.0 / (1.0 + jnp.exp(-a)))
            s_ref[rows, start:start + size] = (silu * b).astype(BF16)
    for rows, x in zip(groups, xs):
        y = jnp.dot(s_ref[rows, :], w2_ref[...], preferred_element_type=F32)
        o_ref[rows, :] = x + 0.5 * _rms(y, g_ref[post:post + 1, :])


def _ffn(x2d, gains, w1, w3, w2, layer, half, mixer=None, *, tm=1024):
    mt, d = x2d.shape
    ff = w1.shape[-1]
    chunk = 4 * LANES
    ff_chunks = tuple((s, min(chunk, ff - s)) for s in range(0, ff, chunk))
    est = (4 * tm * d * 4 + 3 * d * ff * 2 + tm * ff * 2 + 4 * tm * chunk * 4 + 2 * tm * d * 4)
    row = pl.BlockSpec((tm, d), lambda i: (i, 0))
    gain_spec = _resident((None, None, 3, d), lambda i: (layer, half, 0, 0))
    weight_specs = [
        _resident((None, None, d, ff), lambda i: (layer, half, 0, 0)),
        _resident((None, None, d, ff), lambda i: (layer, half, 0, 0)),
        _resident((None, None, ff, d), lambda i: (layer, half, 0, 0)),
    ]
    gains3 = gains.reshape(gains.shape[0], 2, 3, d)
    if mixer is None:
        in_specs = [row, gain_spec] + weight_specs
        operands = (x2d, gains3, w1, w3, w2)
    else:
        attn_out, w_o, w_o_layer = mixer
        est += 2 * tm * d * 2 + d * d * 2 + tm * d * 4
        tiles = attn_out.shape[2] // tm
        attn_spec = pl.BlockSpec((None, d, tm), lambda i: (i // tiles, 0, i % tiles))
        in_specs = ([row, attn_spec, _resident((None, d, d), lambda i: (w_o_layer, 0, 0)), gain_spec]
                    + weight_specs)
        operands = (x2d, attn_out, w_o, gains3, w1, w3, w2)
    return pl.pallas_call(
        functools.partial(_ffn_kernel, ff_chunks=ff_chunks, mixer=mixer is not None),
        out_shape=jax.ShapeDtypeStruct((mt, d), F32),
        grid=(mt // tm,),
        in_specs=in_specs,
        out_specs=row,
        scratch_shapes=[pltpu.VMEM((tm, ff), BF16)],
        compiler_params=pltpu.CompilerParams(
            dimension_semantics=("parallel",), vmem_limit_bytes=_vmem_limit(est)),
        name=f"ffn_l{layer}_h{half}",
    )(*operands)


def _rope_group(x, cos_t, sin_t):
    return x * cos_t + pltpu.roll(x, LANES // 2, 1) * sin_t


def _moba_qkv_kernel(x_ref, g_ref, w_ref, cos_ref, sin_ref, q_ref, k_ref, vt_ref):
    d = D_MODEL
    h = _rms(x_ref[0], g_ref[...]).astype(BF16)
    cos_t = cos_ref[...]
    sin_t = sin_ref[...]
    q = jnp.dot(h, w_ref[:, 0:d], preferred_element_type=F32)
    k = jnp.dot(h, w_ref[:, d:2 * d], preferred_element_type=F32)
    v = jnp.dot(h, w_ref[:, 2 * d:3 * d], preferred_element_type=F32)
    for hd in range(N_HEADS):
        sl = slice(hd * HEAD_DIM, (hd + 1) * HEAD_DIM)
        q_ref[0, :, sl] = _rope_group(q[:, sl], cos_t, sin_t).astype(BF16)
        k_ref[0, :, sl] = _rope_group(k[:, sl], cos_t, sin_t).astype(BF16)
    vt_ref[0] = v.T.astype(BF16)


def _moba_qkv(x3d, gain, w_qkv, layer, cos_t, sin_t, *, tm=1024):
    b, s, d = x3d.shape
    est = 2 * tm * d * 4 + d * 3 * d * 2 + 4 * tm * LANES * 4 + 6 * tm * d * 2 + 4 * tm * d * 4
    return pl.pallas_call(
        _moba_qkv_kernel,
        out_shape=(jax.ShapeDtypeStruct((b, s, d), BF16),
                   jax.ShapeDtypeStruct((b, s, d), BF16),
                   jax.ShapeDtypeStruct((b, d, s), BF16)),
        grid=(b, s // tm),
        in_specs=[
            pl.BlockSpec((1, tm, d), lambda i, j: (i, j, 0)),
            _resident((1, d), lambda i, j: (0, 0)),
            _resident((None, d, 3 * d), lambda i, j: (layer, 0, 0)),
            pl.BlockSpec((tm, LANES), lambda i, j: (j, 0)),
            pl.BlockSpec((tm, LANES), lambda i, j: (j, 0)),
        ],
        out_specs=(pl.BlockSpec((1, tm, d), lambda i, j: (i, j, 0)),
                   pl.BlockSpec((1, tm, d), lambda i, j: (i, j, 0)),
                   pl.BlockSpec((1, d, tm), lambda i, j: (i, 0, j))),
        compiler_params=pltpu.CompilerParams(
            dimension_semantics=("parallel", "parallel"), vmem_limit_bytes=_vmem_limit(est)),
        name="moba_qkv",
    )(x3d, gain.reshape(1, d), w_qkv, cos_t, sin_t)


def _mla_kv_kernel(x_ref, gin_ref, wd_ref, gkv_ref, wu_ref, cos_ref, sin_ref, ka_ref, vt_ref):
    h = _rms(x_ref[0], gin_ref[...]).astype(BF16)
    ckr = jnp.dot(h, wd_ref[...], preferred_element_type=F32)
    c_kv = _rms(ckr[:, :KV_LORA], gkv_ref[...]).astype(BF16)
    k_rope = _rope_group(ckr[:, KV_LORA:], cos_ref[...], sin_ref[...]).astype(BF16)
    kv = jnp.dot(c_kv, wu_ref[...], preferred_element_type=F32)
    for hd in range(N_HEADS):
        base = hd * MLA_QK_PAD
        ka_ref[0, :, base:base + HEAD_DIM] = kv[:, base:base + HEAD_DIM].astype(BF16)
        ka_ref[0, :, base + HEAD_DIM:base + MLA_QK_PAD] = k_rope
        vt_ref[0, hd * HEAD_DIM:(hd + 1) * HEAD_DIM, :] = (
            kv[:, base + HEAD_DIM:base + MLA_QK_PAD].T.astype(BF16))


def _mla_kv(x3d, kv_in_norm, wd_pad, kv_norm, w_ukv, cos_t, sin_t, *, tm=1024):
    b, s, d = x3d.shape
    nd = wd_pad.shape[1]
    nu = w_ukv.shape[1]
    est = 2 * tm * d * 4 + d * nd * 2 + KV_LORA * nu * 2 + 2 * tm * nu * 2 + 2 * tm * d * 2 + 3 * tm * nu * 4
    return pl.pallas_call(
        _mla_kv_kernel,
        out_shape=(jax.ShapeDtypeStruct((b, s, N_HEADS * MLA_QK_PAD), BF16),
                   jax.ShapeDtypeStruct((b, d, s), BF16)),
        grid=(b, s // tm),
        in_specs=[
            pl.BlockSpec((1, tm, d), lambda i, j: (i, j, 0)),
            _resident((1, d), lambda i, j: (0, 0)),
            _resident((d, nd), lambda i, j: (0, 0)),
            _resident((1, KV_LORA), lambda i, j: (0, 0)),
            _resident((KV_LORA, nu), lambda i, j: (0, 0)),
            pl.BlockSpec((tm, LANES), lambda i, j: (j, 0)),
            pl.BlockSpec((tm, LANES), lambda i, j: (j, 0)),
        ],
        out_specs=(pl.BlockSpec((1, tm, N_HEADS * MLA_QK_PAD), lambda i, j: (i, j, 0)),
                   pl.BlockSpec((1, d, tm), lambda i, j: (i, 0, j))),
        compiler_params=pltpu.CompilerParams(
            dimension_semantics=("parallel", "parallel"), vmem_limit_bytes=_vmem_limit(est)),
        name="mla_shared_kv",
    )(x3d, kv_in_norm.reshape(1, d), wd_pad, kv_norm.reshape(1, KV_LORA), w_ukv, cos_t, sin_t)


def _mla_q_kernel(x_ref, g_ref, wd_ref, gq_ref, wu_ref, cos_ref, sin_ref, qa_ref):
    h = _rms(x_ref[0], g_ref[...]).astype(BF16)
    cq = jnp.dot(h, wd_ref[...], preferred_element_type=F32)
    cqn = _rms(cq, gq_ref[...]).astype(BF16)
    qa = jnp.dot(cqn, wu_ref[...], preferred_element_type=F32)
    cos_t = cos_ref[...]
    sin_t = sin_ref[...]
    for hd in range(N_HEADS):
        base = hd * MLA_QK_PAD
        qa_ref[0, :, base:base + HEAD_DIM] = qa[:, base:base + HEAD_DIM].astype(BF16)
        qa_ref[0, :, base + HEAD_DIM:base + MLA_QK_PAD] = _rope_group(
            qa[:, base + HEAD_DIM:base + MLA_QK_PAD], cos_t, sin_t).astype(BF16)


def _mla_q(x3d, gain, w_dq, q_norm, wu_pad, layer, cos_t, sin_t, *, tm=1024):
    b, s, d = x3d.shape
    nu = wu_pad.shape[-1]
    est = 2 * tm * d * 4 + d * Q_LORA * 2 + Q_LORA * nu * 2 + 2 * tm * nu * 2 + 3 * tm * nu * 4
    return pl.pallas_call(
        _mla_q_kernel,
        out_shape=jax.ShapeDtypeStruct((b, s, nu), BF16),
        grid=(b, s // tm),
        in_specs=[
            pl.BlockSpec((1, tm, d), lambda i, j: (i, j, 0)),
            _resident((1, d), lambda i, j: (0, 0)),
            _resident((None, d, Q_LORA), lambda i, j: (layer, 0, 0)),
            _resident((1, Q_LORA), lambda i, j: (0, 0)),
            _resident((None, Q_LORA, nu), lambda i, j: (layer, 0, 0)),
            pl.BlockSpec((tm, LANES), lambda i, j: (j, 0)),
            pl.BlockSpec((tm, LANES), lambda i, j: (j, 0)),
        ],
        out_specs=pl.BlockSpec((1, tm, nu), lambda i, j: (i, j, 0)),
        compiler_params=pltpu.CompilerParams(
            dimension_semantics=("parallel", "parallel"), vmem_limit_bytes=_vmem_limit(est)),
        name="mla_q",
    )(x3d, gain.reshape(1, d), w_dq, q_norm.reshape(1, Q_LORA), wu_pad, cos_t, sin_t)


def _moba_keep(gate, own):
    rows = [gate[n:n + 1, :] for n in range(own)]
    keep = []
    for n in range(own):
        rank = jnp.zeros_like(rows[n])
        for m in range(own):
            if m < n:
                rank = rank + jnp.where(rows[m] >= rows[n], 1.0, 0.0)
            elif m > n:
                rank = rank + jnp.where(rows[m] > rows[n], 1.0, 0.0)
        keep.append(rank < float(MOBA_TOPK))
    return keep


_NT = (((1,), (1,)), ((), ()))


class _Head:
    def __init__(self, q_ref, k_ref, vt_ref, o_ref, idx, dk, dv, moba):
        self.q_ref, self.k_ref, self.vt_ref, self.o_ref = q_ref, k_ref, vt_ref, o_ref
        self.cols = slice(idx * dk, (idx + 1) * dk)
        self.rows = slice(idx * dv, (idx + 1) * dv)
        self.dv, self.moba = dv, moba
        self.blk = MOBA_BLOCK
        self.nb = q_ref.shape[1] // self.blk
        self.ones_rows = jnp.ones((SUM_ROWS, q_ref.shape[1]), BF16)
        if moba:
            sub_i = lax.broadcasted_iota(jnp.int32, (8, dk), 0)
            km = jnp.zeros((8, dk), F32)
            for n in range(self.nb - 1):
                row = jnp.sum(self.k(n).astype(F32), axis=0, keepdims=True)
                km = jnp.where(sub_i == n, row * (1.0 / self.blk), km)
            km_hi = km.astype(BF16)
            km_lo = (km - km_hi.astype(F32)).astype(BF16)
            self.km_hl = jnp.concatenate([km_hi, km_lo], axis=0)

    def q(self, qi):
        return self.q_ref[0, qi * self.blk:(qi + 1) * self.blk, self.cols]

    def k(self, n):
        return self.k_ref[0, n * self.blk:(n + 1) * self.blk, self.cols]

    def keep_rows(self, qi, q_blk):
        if not (self.moba and qi > MOBA_TOPK):
            return None
        g2 = lax.dot_general(self.km_hl, q_blk, _NT, preferred_element_type=F32)
        return _moba_keep(g2[0:8, :] + g2[8:16, :], qi)

    def values(self, qi, p_list):
        p_t = p_list[0] if len(p_list) == 1 else jnp.concatenate(p_list, axis=0)
        kv = (qi + 1) * self.blk
        vt_aug = jnp.concatenate([self.vt_ref[0, self.rows, 0:kv], self.ones_rows[:, 0:kv]], axis=0)
        out_aug = jnp.dot(vt_aug, p_t, preferred_element_type=F32)
        denom = out_aug[self.dv:self.dv + 1, :]
        out_t = out_aug[0:self.dv, :] * (1.0 / denom)
        self.o_ref[0, self.rows, qi * self.blk:(qi + 1) * self.blk] = out_t.astype(BF16)
        return denom


def _attn_kernel(q_ref, k_ref, vt_ref, o_ref, *, moba, exp_scale, heads):
    blk = MOBA_BLOCK
    dk = q_ref.shape[2] // heads
    dv = vt_ref.shape[1] // heads
    key_i = lax.broadcasted_iota(jnp.int32, (blk, blk), 0)
    qry_i = lax.broadcasted_iota(jnp.int32, (blk, blk), 1)
    causal_bias = jnp.where(key_i <= qry_i, 0.0, -jnp.inf)
    head_views = [_Head(q_ref, k_ref, vt_ref, o_ref, i, dk, dv, moba) for i in range(heads)]

    lo, hi = 2.0 ** -DENOM_LOG2_RANGE, 2.0 ** DENOM_LOG2_RANGE
    suspect = None
    for hv in head_views:
        for qi in reversed(range(hv.nb)):
            q_blk = hv.q(qi)
            keep = hv.keep_rows(qi, q_blk)
            p_list = []
            for n in range(qi + 1):
                t = lax.dot_general(hv.k(n), q_blk, _NT, preferred_element_type=F32) * exp_scale
                if n == qi:
                    t = t + causal_bias
                p = jnp.exp2(t).astype(BF16)
                if n < qi and keep is not None:
                    p = p * jnp.where(keep[n], 1.0, 0.0).astype(BF16)
                p_list.append(p)
            denom = hv.values(qi, p_list)
            flag = jnp.where(jnp.logical_and(denom >= lo, denom <= hi), 0.0, 1.0)
            suspect = flag if suspect is None else jnp.maximum(suspect, flag)

    @pl.when(jnp.max(suspect) > 0.0)
    def _():
        for hv in head_views:
            _shifted_softmax_pass(hv, causal_bias, exp_scale)


def _shifted_softmax_pass(hv, causal_bias, exp_scale):
    nb, blk = hv.nb, hv.blk

    def logits_begin(qi):
        q_blk = hv.q(qi)
        keep = hv.keep_rows(qi, q_blk)
        sel_bias = None if keep is None else [jnp.where(kp, 0.0, -jnp.inf) for kp in keep]
        return dict(qi=qi, q=q_blk, bias=sel_bias, s=[], m8=None)

    def logits_step(st, n):
        qi = st["qi"]
        s = lax.dot_general(hv.k(n), st["q"], _NT, preferred_element_type=F32)
        if n == qi:
            s = s + causal_bias
        st["s"].append(s)
        bm8 = jnp.max(s.reshape(blk // 8, 8, blk), axis=0)
        if n < qi and st["bias"] is not None:
            bm8 = bm8 + st["bias"][n]
        st["m8"] = bm8 if st["m8"] is None else jnp.maximum(st["m8"], bm8)

    def weights_step(st, m, n):
        m_n = m - st["bias"][n] if (n < st["qi"] and st["bias"] is not None) else m
        return jnp.exp2((st["s"][n] - m_n) * exp_scale).astype(BF16)

    def logits_all(qi):
        st = logits_begin(qi)
        for n in range(qi + 1):
            logits_step(st, n)
        return st

    ahead = [logits_all(qi) for qi in range(min(ATTN_LOOKAHEAD, nb))]
    for qi in range(nb):
        if qi + ATTN_LOOKAHEAD < nb:
            ahead.append(logits_all(qi + ATTN_LOOKAHEAD))
        cur = ahead.pop(0)
        m = jnp.max(cur["m8"], axis=0, keepdims=True)
        hv.values(qi, [weights_step(cur, m, n) for n in range(qi + 1)])


def _attention(q, k, vt, *, moba, qk_dim, scale, heads=ATTN_HEADS_PER_STEP):
    b, s, _ = q.shape
    dv = HEAD_DIM
    est = heads * (2 * (2 * s * qk_dim * 2 + 2 * s * dv * 2) + 24 * MOBA_BLOCK * s * 4)
    return pl.pallas_call(
        functools.partial(_attn_kernel, moba=moba, exp_scale=scale * LOG2_E, heads=heads),
        out_shape=jax.ShapeDtypeStruct((b, N_HEADS * dv, s), BF16),
        grid=(b, N_HEADS // heads),
        in_specs=[
            pl.BlockSpec((1, s, heads * qk_dim), lambda i, j: (i, 0, j)),
            pl.BlockSpec((1, s, heads * qk_dim), lambda i, j: (i, 0, j)),
            pl.BlockSpec((1, heads * dv, s), lambda i, j: (i, j, 0)),
        ],
        out_specs=pl.BlockSpec((1, heads * dv, s), lambda i, j: (i, j, 0)),
        compiler_params=pltpu.CompilerParams(
            dimension_semantics=("parallel", "parallel"), vmem_limit_bytes=_vmem_limit(est)),
        name="moba_attn" if moba else "mla_attn",
    )(q, k, vt)


def _rope_tables(seq_len, dim):
    inv_freq = 1.0 / (ROPE_THETA ** (jnp.arange(0, dim, 2, dtype=F32) / dim))
    ang = jnp.arange(seq_len, dtype=F32)[:, None] * inv_freq[None, :]
    return jnp.cos(ang), jnp.sin(ang)


def _rope_group_layout(main, rope):
    half = ROPE_DIM // 2
    zeros = jnp.zeros(rope.shape[:-1] + (LANES // 2 - half,), rope.dtype)
    return jnp.concatenate([main, rope[..., :half], zeros, rope[..., half:], zeros], axis=-1)


def kernel(x, norm_gains, ffn_w1, ffn_w3, ffn_w2, moba_w_qkv, moba_w_o, mla_w_dq, mla_q_norm,
           mla_w_uq, mla_w_o, kv_in_norm, w_dkv_kr, kv_norm, w_ukv):
    b, s, d = x.shape
    depth = norm_gains.shape[0]
    n_a = moba_w_qkv.shape[0]

    cos_a, sin_a = _rope_tables(s, HEAD_DIM)
    cos_moba = jnp.concatenate([cos_a, cos_a], axis=1)
    sin_moba = jnp.concatenate([-sin_a, sin_a], axis=1)
    cos_r, sin_r = _rope_tables(s, ROPE_DIM)
    zpad = jnp.zeros_like(cos_r)
    cos_mla = jnp.concatenate([cos_r, zpad, cos_r, zpad], axis=1)
    sin_mla = jnp.concatenate([-sin_r, zpad, sin_r, zpad], axis=1)

    w1 = ffn_w1.astype(BF16)
    w3 = ffn_w3.astype(BF16)
    w2 = ffn_w2.astype(BF16)
    w_qkv = moba_w_qkv.astype(BF16)
    w_o_a = moba_w_o.astype(BF16)
    w_o_b = mla_w_o.astype(BF16)
    w_dq = mla_w_dq.astype(BF16)
    wd_pad = _rope_group_layout(w_dkv_kr[:, :KV_LORA], w_dkv_kr[:, KV_LORA:]).astype(BF16)
    uq = mla_w_uq.reshape(mla_w_uq.shape[0], Q_LORA, N_HEADS, HEAD_DIM + ROPE_DIM)
    wu_pad = _rope_group_layout(uq[..., :HEAD_DIM], uq[..., HEAD_DIM:])
    wu_pad = wu_pad.reshape(mla_w_uq.shape[0], Q_LORA, N_HEADS * MLA_QK_PAD).astype(BF16)
    w_ukv_b = w_ukv.astype(BF16)

    x2 = x.reshape(b * s, d)
    k_aug = vt_shared = None
    for l in range(depth):
        g = norm_gains[l]
        x2 = _ffn(x2, norm_gains, w1, w3, w2, l, 0)
        x3 = x2.reshape(b, s, d)
        if l < n_a:
            q, k, vt = _moba_qkv(x3, g[2], w_qkv, l, cos_moba, sin_moba)
            o = _attention(q, k, vt, moba=True, qk_dim=HEAD_DIM, scale=HEAD_DIM ** -0.5)
            mixer = (o, w_o_a, l)
        else:
            j = l - n_a
            q_aug = _mla_q(x3, g[2], w_dq, mla_q_norm[j], wu_pad, j, cos_mla, sin_mla)
            o = _attention(q_aug, k_aug, vt_shared, moba=False, qk_dim=MLA_QK_PAD,
                           scale=(HEAD_DIM + ROPE_DIM) ** -0.5)
            mixer = (o, w_o_b, j)
        x2 = _ffn(x2, norm_gains, w1, w3, w2, l, 1, mixer)
        if l == n_a - 1:
            k_aug, vt_shared = _mla_kv(x2.reshape(b, s, d), kv_in_norm, wd_pad, kv_norm,
                                       w_ukv_b, cos_mla, sin_mla)
    return x2.reshape(b, s, d)
```

```python
import functools

import jax
import jax.numpy as jnp
from jax import lax
from jax.experimental import pallas as pl
from jax.experimental.pallas import tpu as pltpu

D_MODEL = 1024
N_HEADS = 8
HEAD_DIM = 128
MOBA_BLOCK = 256
MOBA_TOPK = 3
ROPE_DIM = 64
KV_LORA = 256
Q_LORA = 512
ROPE_THETA = 10000.0
NORM_EPS = 1e-6
LANES = 128
MLA_QK_PAD = 2 * LANES
SUM_ROWS = 16
MIXER_ROW_GROUPS = 2
QK_BLOCKS_PER_DOT = 2
ATTN_HEADS_PER_STEP = 4
ATTN_LOOKAHEAD = 3
DENOM_LOG2_RANGE = 60.0
V7X_VMEM_BYTES = 64 * 1024 * 1024
LOG2_E = 1.4426950408889634

F32 = jnp.float32
BF16 = jnp.bfloat16


def _vmem_limit(estimate_bytes):
    return int(min(estimate_bytes * 3 // 2, V7X_VMEM_BYTES * 15 // 16))


def _rms(x, g):
    ms = jnp.mean(x * x, axis=-1, keepdims=True)
    return x * lax.rsqrt(ms + NORM_EPS) * g


def _resident(shape, index_map):
    return pl.BlockSpec(shape, index_map, pipeline_mode=pl.Buffered(1))


def _ffn_kernel(*refs, ff_chunks, mixer):
    if mixer:
        x_ref, a_ref, wo_ref, g_ref, w1_ref, w3_ref, w2_ref, o_ref, s_ref = refs
        pre, post = 1, 2
    else:
        x_ref, g_ref, w1_ref, w3_ref, w2_ref, o_ref, s_ref = refs
        pre, post = 0, 1
    tm = x_ref.shape[0]
    n_groups = MIXER_ROW_GROUPS if mixer else 1
    groups = [slice(r, r + tm // n_groups) for r in range(0, tm, tm // n_groups)]
    xs, hs = [], []
    for rows in groups:
        x = x_ref[rows, :]
        if mixer:
            mix = lax.dot_general(a_ref[:, rows], wo_ref[...], (((0,), (0,)), ((), ())),
                                  preferred_element_type=F32)
            x = x + _rms(mix, g_ref[0:1, :])
        xs.append(x)
        hs.append(_rms(x, g_ref[pre:pre + 1, :]).astype(BF16))
    for rows, h in zip(groups, hs):
        for start, size in ff_chunks:
            a = jnp.dot(h, w1_ref[:, start:start + size], preferred_element_type=F32)
            b = jnp.dot(h, w3_ref[:, start:start + size], preferred_element_type=F32)
            silu = a * (1.0 / (1.0 + jnp.exp(-a)))
            s_ref[rows, start:start + size] = (silu * b).astype(BF16)
    for rows, x in zip(groups, xs):
        y = jnp.dot(s_ref[rows, :], w2_ref[...], preferred_element_type=F32)
        o_ref[rows, :] = x + 0.5 * _rms(y, g_ref[post:post + 1, :])


def _ffn(x2d, gains, w1, w3, w2, layer, half, mixer=None, *, tm=1024):
    mt, d = x2d.shape
    ff = w1.shape[-1]
    chunk = 4 * LANES
    ff_chunks = tuple((s, min(chunk, ff - s)) for s in range(0, ff, chunk))
    est = (4 * tm * d * 4 + 3 * d * ff * 2 + tm * ff * 2 + 4 * tm * chunk * 4 + 2 * tm * d * 4)
    row = pl.BlockSpec((tm, d), lambda i: (i, 0))
    gain_spec = _resident((None, None, 3, d), lambda i: (layer, half, 0, 0))
    weight_specs = [
        _resident((None, None, d, ff), lambda i: (layer, half, 0, 0)),
        _resident((None, None, d, ff), lambda i: (layer, half, 0, 0)),
        _resident((None, None, ff, d), lambda i: (layer, half, 0, 0)),
    ]
    gains3 = gains.reshape(gains.shape[0], 2, 3, d)
    if mixer is None:
        in_specs = [row, gain_spec] + weight_specs
        operands = (x2d, gains3, w1, w3, w2)
    else:
        attn_out, w_o, w_o_layer = mixer
        est += 2 * tm * d * 2 + d * d * 2 + tm * d * 4
        tiles = attn_out.shape[2] // tm
        attn_spec = pl.BlockSpec((None, d, tm), lambda i: (i // tiles, 0, i % tiles))
        in_specs = ([row, attn_spec, _resident((None, d, d), lambda i: (w_o_layer, 0, 0)), gain_spec]
                    + weight_specs)
        operands = (x2d, attn_out, w_o, gains3, w1, w3, w2)
    return pl.pallas_call(
        functools.partial(_ffn_kernel, ff_chunks=ff_chunks, mixer=mixer is not None),
        out_shape=jax.ShapeDtypeStruct((mt, d), F32),
        grid=(mt // tm,),
        in_specs=in_specs,
        out_specs=row,
        scratch_shapes=[pltpu.VMEM((tm, ff), BF16)],
        compiler_params=pltpu.CompilerParams(
            dimension_semantics=("parallel",), vmem_limit_bytes=_vmem_limit(est)),
        name=f"ffn_l{layer}_h{half}",
    )(*operands)


def _rope_group(x, cos_t, sin_t):
    return x * cos_t + pltpu.roll(x, LANES // 2, 1) * sin_t


def _moba_qkv_kernel(x_ref, g_ref, w_ref, cos_ref, sin_ref, q_ref, k_ref, vt_ref):
    d = D_MODEL
    h = _rms(x_ref[0], g_ref[...]).astype(BF16)
    cos_t = cos_ref[...]
    sin_t = sin_ref[...]
    q = jnp.dot(h, w_ref[:, 0:d], preferred_element_type=F32)
    k = jnp.dot(h, w_ref[:, d:2 * d], preferred_element_type=F32)
    v = jnp.dot(h, w_ref[:, 2 * d:3 * d], preferred_element_type=F32)
    for hd in range(N_HEADS):
        sl = slice(hd * HEAD_DIM, (hd + 1) * HEAD_DIM)
        q_ref[0, :, sl] = _rope_group(q[:, sl], cos_t, sin_t).astype(BF16)
        k_ref[0, :, sl] = _rope_group(k[:, sl], cos_t, sin_t).astype(BF16)
    vt_ref[0] = v.T.astype(BF16)


def _moba_qkv(x3d, gain, w_qkv, layer, cos_t, sin_t, *, tm=1024):
    b, s, d = x3d.shape
    est = 2 * tm * d * 4 + d * 3 * d * 2 + 4 * tm * LANES * 4 + 6 * tm * d * 2 + 4 * tm * d * 4
    return pl.pallas_call(
        _moba_qkv_kernel,
        out_shape=(jax.ShapeDtypeStruct((b, s, d), BF16),
                   jax.ShapeDtypeStruct((b, s, d), BF16),
                   jax.ShapeDtypeStruct((b, d, s), BF16)),
        grid=(b, s // tm),
        in_specs=[
            pl.BlockSpec((1, tm, d), lambda i, j: (i, j, 0)),
            _resident((1, d), lambda i, j: (0, 0)),
            _resident((None, d, 3 * d), lambda i, j: (layer, 0, 0)),
            pl.BlockSpec((tm, LANES), lambda i, j: (j, 0)),
            pl.BlockSpec((tm, LANES), lambda i, j: (j, 0)),
        ],
        out_specs=(pl.BlockSpec((1, tm, d), lambda i, j: (i, j, 0)),
                   pl.BlockSpec((1, tm, d), lambda i, j: (i, j, 0)),
                   pl.BlockSpec((1, d, tm), lambda i, j: (i, 0, j))),
        compiler_params=pltpu.CompilerParams(
            dimension_semantics=("parallel", "parallel"), vmem_limit_bytes=_vmem_limit(est)),
        name="moba_qkv",
    )(x3d, gain.reshape(1, d), w_qkv, cos_t, sin_t)


def _mla_kv_kernel(x_ref, gin_ref, wd_ref, gkv_ref, wu_ref, cos_ref, sin_ref, ka_ref, vt_ref):
    h = _rms(x_ref[0], gin_ref[...]).astype(BF16)
    ckr = jnp.dot(h, wd_ref[...], preferred_element_type=F32)
    c_kv = _rms(ckr[:, :KV_LORA], gkv_ref[...]).astype(BF16)
    k_rope = _rope_group(ckr[:, KV_LORA:], cos_ref[...], sin_ref[...]).astype(BF16)
    kv = jnp.dot(c_kv, wu_ref[...], preferred_element_type=F32)
    for hd in range(N_HEADS):
        base = hd * MLA_QK_PAD
        ka_ref[0, :, base:base + HEAD_DIM] = kv[:, base:base + HEAD_DIM].astype(BF16)
        ka_ref[0, :, base + HEAD_DIM:base + MLA_QK_PAD] = k_rope
        vt_ref[0, hd * HEAD_DIM:(hd + 1) * HEAD_DIM, :] = (
            kv[:, base + HEAD_DIM:base + MLA_QK_PAD].T.astype(BF16))


def _mla_kv(x3d, kv_in_norm, wd_pad, kv_norm, w_ukv, cos_t, sin_t, *, tm=1024):
    b, s, d = x3d.shape
    nd = wd_pad.shape[1]
    nu = w_ukv.shape[1]
    est = 2 * tm * d * 4 + d * nd * 2 + KV_LORA * nu * 2 + 2 * tm * nu * 2 + 2 * tm * d * 2 + 3 * tm * nu * 4
    return pl.pallas_call(
        _mla_kv_kernel,
        out_shape=(jax.ShapeDtypeStruct((b, s, N_HEADS * MLA_QK_PAD), BF16),
                   jax.ShapeDtypeStruct((b, d, s), BF16)),
        grid=(b, s // tm),
        in_specs=[
            pl.BlockSpec((1, tm, d), lambda i, j: (i, j, 0)),
            _resident((1, d), lambda i, j: (0, 0)),
            _resident((d, nd), lambda i, j: (0, 0)),
            _resident((1, KV_LORA), lambda i, j: (0, 0)),
            _resident((KV_LORA, nu), lambda i, j: (0, 0)),
            pl.BlockSpec((tm, LANES), lambda i, j: (j, 0)),
            pl.BlockSpec((tm, LANES), lambda i, j: (j, 0)),
        ],
        out_specs=(pl.BlockSpec((1, tm, N_HEADS * MLA_QK_PAD), lambda i, j: (i, j, 0)),
                   pl.BlockSpec((1, d, tm), lambda i, j: (i, 0, j))),
        compiler_params=pltpu.CompilerParams(
            dimension_semantics=("parallel", "parallel"), vmem_limit_bytes=_vmem_limit(est)),
        name="mla_shared_kv",
    )(x3d, kv_in_norm.reshape(1, d), wd_pad, kv_norm.reshape(1, KV_LORA), w_ukv, cos_t, sin_t)


def _mla_q_kernel(x_ref, g_ref, wd_ref, gq_ref, wu_ref, cos_ref, sin_ref, qa_ref):
    h = _rms(x_ref[0], g_ref[...]).astype(BF16)
    cq = jnp.dot(h, wd_ref[...], preferred_element_type=F32)
    cqn = _rms(cq, gq_ref[...]).astype(BF16)
    qa = jnp.dot(cqn, wu_ref[...], preferred_element_type=F32)
    cos_t = cos_ref[...]
    sin_t = sin_ref[...]
    for hd in range(N_HEADS):
        base = hd * MLA_QK_PAD
        qa_ref[0, :, base:base + HEAD_DIM] = qa[:, base:base + HEAD_DIM].astype(BF16)
        qa_ref[0, :, base + HEAD_DIM:base + MLA_QK_PAD] = _rope_group(
            qa[:, base + HEAD_DIM:base + MLA_QK_PAD], cos_t, sin_t).astype(BF16)


def _mla_q(x3d, gain, w_dq, q_norm, wu_pad, layer, cos_t, sin_t, *, tm=1024):
    b, s, d = x3d.shape
    nu = wu_pad.shape[-1]
    est = 2 * tm * d * 4 + d * Q_LORA * 2 + Q_LORA * nu * 2 + 2 * tm * nu * 2 + 3 * tm * nu * 4
    return pl.pallas_call(
        _mla_q_kernel,
        out_shape=jax.ShapeDtypeStruct((b, s, nu), BF16),
        grid=(b, s // tm),
        in_specs=[
            pl.BlockSpec((1, tm, d), lambda i, j: (i, j, 0)),
            _resident((1, d), lambda i, j: (0, 0)),
            _resident((None, d, Q_LORA), lambda i, j: (layer, 0, 0)),
            _resident((1, Q_LORA), lambda i, j: (0, 0)),
            _resident((None, Q_LORA, nu), lambda i, j: (layer, 0, 0)),
            pl.BlockSpec((tm, LANES), lambda i, j: (j, 0)),
            pl.BlockSpec((tm, LANES), lambda i, j: (j, 0)),
        ],
        out_specs=pl.BlockSpec((1, tm, nu), lambda i, j: (i, j, 0)),
        compiler_params=pltpu.CompilerParams(
            dimension_semantics=("parallel", "parallel"), vmem_limit_bytes=_vmem_limit(est)),
        name="mla_q",
    )(x3d, gain.reshape(1, d), w_dq, q_norm.reshape(1, Q_LORA), wu_pad, cos_t, sin_t)


def _moba_keep(gate, own):
    rows = [gate[n:n + 1, :] for n in range(own)]
    keep = []
    for n in range(own):
        rank = jnp.zeros_like(rows[n])
        for m in range(own):
            if m < n:
                rank = rank + jnp.where(rows[m] >= rows[n], 1.0, 0.0)
            elif m > n:
                rank = rank + jnp.where(rows[m] > rows[n], 1.0, 0.0)
        keep.append(rank < float(MOBA_TOPK))
    return keep


_NT = (((1,), (1,)), ((), ()))


class _Head:
    def __init__(self, q_ref, k_ref, vt_ref, o_ref, idx, dk, dv, moba):
        self.q_ref, self.k_ref, self.vt_ref, self.o_ref = q_ref, k_ref, vt_ref, o_ref
        self.cols = slice(idx * dk, (idx + 1) * dk)
        self.rows = slice(idx * dv, (idx + 1) * dv)
        self.dv, self.moba = dv, moba
        self.blk = MOBA_BLOCK
        self.nb = q_ref.shape[1] // self.blk
        self.ones_rows = jnp.ones((SUM_ROWS, q_ref.shape[1]), BF16)
        if moba:
            sub_i = lax.broadcasted_iota(jnp.int32, (8, dk), 0)
            km = jnp.zeros((8, dk), F32)
            for n in range(self.nb - 1):
                row = jnp.sum(self.k(n).astype(F32), axis=0, keepdims=True)
                km = jnp.where(sub_i == n, row * (1.0 / self.blk), km)
            km_hi = km.astype(BF16)
            km_lo = (km - km_hi.astype(F32)).astype(BF16)
            self.km_hl = jnp.concatenate([km_hi, km_lo], axis=0)

    def q(self, qi):
        return self.q_ref[0, qi * self.blk:(qi + 1) * self.blk, self.cols]

    def k(self, n):
        return self.k_ref[0, n * self.blk:(n + 1) * self.blk, self.cols]

    def k_span(self, n0, count):
        return self.k_ref[0, n0 * self.blk:(n0 + count) * self.blk, self.cols]

    def keep_rows(self, qi, q_blk):
        if not (self.moba and qi > MOBA_TOPK):
            return None
        g2 = lax.dot_general(self.km_hl, q_blk, _NT, preferred_element_type=F32)
        return _moba_keep(g2[0:8, :] + g2[8:16, :], qi)

    def values(self, qi, p_list):
        p_t = p_list[0] if len(p_list) == 1 else jnp.concatenate(p_list, axis=0)
        kv = (qi + 1) * self.blk
        vt_aug = jnp.concatenate([self.vt_ref[0, self.rows, 0:kv], self.ones_rows[:, 0:kv]], axis=0)
        out_aug = jnp.dot(vt_aug, p_t, preferred_element_type=F32)
        denom = out_aug[self.dv:self.dv + 1, :]
        out_t = out_aug[0:self.dv, :] * (1.0 / denom)
        self.o_ref[0, self.rows, qi * self.blk:(qi + 1) * self.blk] = out_t.astype(BF16)
        return denom


def _attn_kernel(q_ref, k_ref, vt_ref, o_ref, *, moba, exp_scale, heads):
    blk = MOBA_BLOCK
    dk = q_ref.shape[2] // heads
    dv = vt_ref.shape[1] // heads
    key_i = lax.broadcasted_iota(jnp.int32, (blk, blk), 0)
    qry_i = lax.broadcasted_iota(jnp.int32, (blk, blk), 1)
    causal_bias = jnp.where(key_i <= qry_i, 0.0, -jnp.inf)
    head_views = [_Head(q_ref, k_ref, vt_ref, o_ref, i, dk, dv, moba) for i in range(heads)]

    lo, hi = 2.0 ** -DENOM_LOG2_RANGE, 2.0 ** DENOM_LOG2_RANGE
    suspect = None
    for hv in head_views:
        for qi in reversed(range(hv.nb)):
            q_blk = hv.q(qi)
            keep = hv.keep_rows(qi, q_blk)
            p_list = []
            t_blocks = []
            for n0 in range(0, qi + 1, QK_BLOCKS_PER_DOT):
                cnt = min(QK_BLOCKS_PER_DOT, qi + 1 - n0)
                t_grp = lax.dot_general(hv.k_span(n0, cnt), q_blk, _NT,
                                        preferred_element_type=F32) * exp_scale
                t_blocks += [t_grp[i * blk:(i + 1) * blk, :] for i in range(cnt)]
            for n in range(qi + 1):
                t = t_blocks[n]
                if n == qi:
                    t = t + causal_bias
                p = jnp.exp2(t).astype(BF16)
                if n < qi and keep is not None:
                    p = p * jnp.where(keep[n], 1.0, 0.0).astype(BF16)
                p_list.append(p)
            denom = hv.values(qi, p_list)
            flag = jnp.where(jnp.logical_and(denom >= lo, denom <= hi), 0.0, 1.0)
            suspect = flag if suspect is None else jnp.maximum(suspect, flag)

    @pl.when(jnp.max(suspect) > 0.0)
    def _():
        for hv in head_views:
            _shifted_softmax_pass(hv, causal_bias, exp_scale)


def _shifted_softmax_pass(hv, causal_bias, exp_scale):
    nb, blk = hv.nb, hv.blk

    def logits_begin(qi):
        q_blk = hv.q(qi)
        keep = hv.keep_rows(qi, q_blk)
        sel_bias = None if keep is None else [jnp.where(kp, 0.0, -jnp.inf) for kp in keep]
        return dict(qi=qi, q=q_blk, bias=sel_bias, s=[], m8=None)

    def logits_step(st, n):
        qi = st["qi"]
        s = lax.dot_general(hv.k(n), st["q"], _NT, preferred_element_type=F32)
        if n == qi:
            s = s + causal_bias
        st["s"].append(s)
        bm8 = jnp.max(s.reshape(blk // 8, 8, blk), axis=0)
        if n < qi and st["bias"] is not None:
            bm8 = bm8 + st["bias"][n]
        st["m8"] = bm8 if st["m8"] is None else jnp.maximum(st["m8"], bm8)

    def weights_step(st, m, n):
        m_n = m - st["bias"][n] if (n < st["qi"] and st["bias"] is not None) else m
        return jnp.exp2((st["s"][n] - m_n) * exp_scale).astype(BF16)

    def logits_all(qi):
        st = logits_begin(qi)
        for n in range(qi + 1):
            logits_step(st, n)
        return st

    ahead = [logits_all(qi) for qi in range(min(ATTN_LOOKAHEAD, nb))]
    for qi in range(nb):
        if qi + ATTN_LOOKAHEAD < nb:
            ahead.append(logits_all(qi + ATTN_LOOKAHEAD))
        cur = ahead.pop(0)
        m = jnp.max(cur["m8"], axis=0, keepdims=True)
        hv.values(qi, [weights_step(cur, m, n) for n in range(qi + 1)])


def _attention(q, k, vt, *, moba, qk_dim, scale, heads=ATTN_HEADS_PER_STEP):
    b, s, _ = q.shape
    dv = HEAD_DIM
    est = heads * (2 * (2 * s * qk_dim * 2 + 2 * s * dv * 2) + 24 * MOBA_BLOCK * s * 4)
    return pl.pallas_call(
        functools.partial(_attn_kernel, moba=moba, exp_scale=scale * LOG2_E, heads=heads),
        out_shape=jax.ShapeDtypeStruct((b, N_HEADS * dv, s), BF16),
        grid=(b, N_HEADS // heads),
        in_specs=[
            pl.BlockSpec((1, s, heads * qk_dim), lambda i, j: (i, 0, j)),
            pl.BlockSpec((1, s, heads * qk_dim), lambda i, j: (i, 0, j)),
            pl.BlockSpec((1, heads * dv, s), lambda i, j: (i, j, 0)),
        ],
        out_specs=pl.BlockSpec((1, heads * dv, s), lambda i, j: (i, j, 0)),
        compiler_params=pltpu.CompilerParams(
            dimension_semantics=("parallel", "parallel"), vmem_limit_bytes=_vmem_limit(est)),
        name="moba_attn" if moba else "mla_attn",
    )(q, k, vt)


def _rope_tables(seq_len, dim):
    inv_freq = 1.0 / (ROPE_THETA ** (jnp.arange(0, dim, 2, dtype=F32) / dim))
    ang = jnp.arange(seq_len, dtype=F32)[:, None] * inv_freq[None, :]
    return jnp.cos(ang), jnp.sin(ang)


def _rope_group_layout(main, rope):
    half = ROPE_DIM // 2
    zeros = jnp.zeros(rope.shape[:-1] + (LANES // 2 - half,), rope.dtype)
    return jnp.concatenate([main, rope[..., :half], zeros, rope[..., half:], zeros], axis=-1)


def kernel(x, norm_gains, ffn_w1, ffn_w3, ffn_w2, moba_w_qkv, moba_w_o, mla_w_dq, mla_q_norm,
           mla_w_uq, mla_w_o, kv_in_norm, w_dkv_kr, kv_norm, w_ukv):
    b, s, d = x.shape
    depth = norm_gains.shape[0]
    n_a = moba_w_qkv.shape[0]

    cos_a, sin_a = _rope_tables(s, HEAD_DIM)
    cos_moba = jnp.concatenate([cos_a, cos_a], axis=1)
    sin_moba = jnp.concatenate([-sin_a, sin_a], axis=1)
    cos_r, sin_r = _rope_tables(s, ROPE_DIM)
    zpad = jnp.zeros_like(cos_r)
    cos_mla = jnp.concatenate([cos_r, zpad, cos_r, zpad], axis=1)
    sin_mla = jnp.concatenate([-sin_r, zpad, sin_r, zpad], axis=1)

    w1 = ffn_w1.astype(BF16)
    w3 = ffn_w3.astype(BF16)
    w2 = ffn_w2.astype(BF16)
    w_qkv = moba_w_qkv.astype(BF16)
    w_o_a = moba_w_o.astype(BF16)
    w_o_b = mla_w_o.astype(BF16)
    w_dq = mla_w_dq.astype(BF16)
    wd_pad = _rope_group_layout(w_dkv_kr[:, :KV_LORA], w_dkv_kr[:, KV_LORA:]).astype(BF16)
    uq = mla_w_uq.reshape(mla_w_uq.shape[0], Q_LORA, N_HEADS, HEAD_DIM + ROPE_DIM)
    wu_pad = _rope_group_layout(uq[..., :HEAD_DIM], uq[..., HEAD_DIM:])
    wu_pad = wu_pad.reshape(mla_w_uq.shape[0], Q_LORA, N_HEADS * MLA_QK_PAD).astype(BF16)
    w_ukv_b = w_ukv.astype(BF16)

    x2 = x.reshape(b * s, d)
    k_aug = vt_shared = None
    for l in range(depth):
        g = norm_gains[l]
        x2 = _ffn(x2, norm_gains, w1, w3, w2, l, 0)
        x3 = x2.reshape(b, s, d)
        if l < n_a:
            q, k, vt = _moba_qkv(x3, g[2], w_qkv, l, cos_moba, sin_moba)
            o = _attention(q, k, vt, moba=True, qk_dim=HEAD_DIM, scale=HEAD_DIM ** -0.5)
            mixer = (o, w_o_a, l)
        else:
            j = l - n_a
            q_aug = _mla_q(x3, g[2], w_dq, mla_q_norm[j], wu_pad, j, cos_mla, sin_mla)
            o = _attention(q_aug, k_aug, vt_shared, moba=False, qk_dim=MLA_QK_PAD,
                           scale=(HEAD_DIM + ROPE_DIM) ** -0.5)
            mixer = (o, w_o_b, j)
        x2 = _ffn(x2, norm_gains, w1, w3, w2, l, 1, mixer)
        if l == n_a - 1:
            k_aug, vt_shared = _mla_kv(x2.reshape(b, s, d), kv_in_norm, wd_pad, kv_norm,
                                       w_ukv_b, cos_mla, sin_mla)
    return x2.reshape(b, s, d)
```

```python
import functools

import jax
import jax.numpy as jnp
from jax import lax
from jax.experimental import pallas as pl
from jax.experimental.pallas import tpu as pltpu

D_MODEL = 1024
N_HEADS = 8
HEAD_DIM = 128
MOBA_BLOCK = 256
MOBA_TOPK = 3
ROPE_DIM = 64
KV_LORA = 256
Q_LORA = 512
ROPE_THETA = 10000.0
NORM_EPS = 1e-6
LANES = 128
MLA_QK_PAD = 2 * LANES
SUM_ROWS = 16
WEIGHT_LOAD_CHUNKS = 16
MIXER_ROW_GROUPS = 2
QK_BLOCKS_PER_DOT = 2
ATTN_HEADS_PER_STEP = 4
ATTN_LOOKAHEAD = 3
DENOM_LOG2_RANGE = 60.0
V7X_VMEM_BYTES = 64 * 1024 * 1024
LOG2_E = 1.4426950408889634

F32 = jnp.float32
BF16 = jnp.bfloat16


def _vmem_limit(estimate_bytes):
    return int(min(estimate_bytes * 3 // 2, V7X_VMEM_BYTES * 15 // 16))


def _rms(x, g):
    ms = jnp.mean(x * x, axis=-1, keepdims=True)
    return x * lax.rsqrt(ms + NORM_EPS) * g


def _resident(shape, index_map):
    return pl.BlockSpec(shape, index_map, pipeline_mode=pl.Buffered(1))


def _load_weights_as_bf16(jobs):
    slots, copies = [], []
    use_count = {}
    for src, stage, sems, _ in jobs:
        slot = use_count.get(id(stage), 0) % 2
        use_count[id(stage)] = use_count.get(id(stage), 0) + 1
        slots.append(slot)
        copies.append(pltpu.make_async_copy(src, stage.at[slot], sems.at[slot]))
    copies[0].start()
    for c, (_, stage, _, dst) in enumerate(jobs):
        if c + 1 < len(jobs):
            copies[c + 1].start()
        copies[c].wait()
        dst[...] = stage[slots[c]].astype(BF16)


def _ffn_kernel(*refs, ff_chunks, mixer, layer, half):
    if mixer:
        (x_ref, a_ref, wo_ref, g_ref, w1_hbm, w3_hbm, w2_hbm, o_ref,
         s_ref, w1_ref, w3_ref, w2_ref, up_stage, down_stage, up_sems, down_sems) = refs
        pre, post = 1, 2
    else:
        (x_ref, g_ref, w1_hbm, w3_hbm, w2_hbm, o_ref,
         s_ref, w1_ref, w3_ref, w2_ref, up_stage, down_stage, up_sems, down_sems) = refs
        pre, post = 0, 1

    @pl.when(pl.program_id(0) == 0)
    def _():
        up_rows, down_rows = up_stage.shape[1], down_stage.shape[1]
        jobs = []
        for hbm, dst in ((w1_hbm, w1_ref), (w3_hbm, w3_ref)):
            for r in range(0, dst.shape[0], up_rows):
                jobs.append((hbm.at[layer, half, r:r + up_rows, :], up_stage, up_sems,
                             dst.at[r:r + up_rows, :]))
        for r in range(0, w2_ref.shape[0], down_rows):
            jobs.append((w2_hbm.at[layer, half, r:r + down_rows, :], down_stage, down_sems,
                         w2_ref.at[r:r + down_rows, :]))
        _load_weights_as_bf16(jobs)

    tm = x_ref.shape[0]
    n_groups = MIXER_ROW_GROUPS if mixer else 1
    groups = [slice(r, r + tm // n_groups) for r in range(0, tm, tm // n_groups)]
    xs, hs = [], []
    for rows in groups:
        x = x_ref[rows, :]
        if mixer:
            mix = lax.dot_general(a_ref[:, rows], wo_ref[...], (((0,), (0,)), ((), ())),
                                  preferred_element_type=F32)
            x = x + _rms(mix, g_ref[0:1, :])
        xs.append(x)
        hs.append(_rms(x, g_ref[pre:pre + 1, :]).astype(BF16))
    for rows, h in zip(groups, hs):
        for start, size in ff_chunks:
            a = jnp.dot(h, w1_ref[:, start:start + size], preferred_element_type=F32)
            b = jnp.dot(h, w3_ref[:, start:start + size], preferred_element_type=F32)
            silu = a * (1.0 / (1.0 + jnp.exp(-a)))
            s_ref[rows, start:start + size] = (silu * b).astype(BF16)
    for rows, x in zip(groups, xs):
        y = jnp.dot(s_ref[rows, :], w2_ref[...], preferred_element_type=F32)
        o_ref[rows, :] = x + 0.5 * _rms(y, g_ref[post:post + 1, :])


def _ffn(x2d, gains, w1, w3, w2, layer, half, mixer=None, *, tm=1024):
    mt, d = x2d.shape
    ff = w1.shape[-1]
    chunk = 4 * LANES
    ff_chunks = tuple((s, min(chunk, ff - s)) for s in range(0, ff, chunk))
    up_rows, down_rows = d // WEIGHT_LOAD_CHUNKS, ff // WEIGHT_LOAD_CHUNKS
    stage_bytes = 2 * (up_rows * ff + down_rows * d) * 4
    est = (4 * tm * d * 4 + 3 * d * ff * 2 + tm * ff * 2 + 4 * tm * chunk * 4 + 2 * tm * d * 4
           + stage_bytes)
    row = pl.BlockSpec((tm, d), lambda i: (i, 0))
    gain_spec = _resident((None, None, 3, d), lambda i: (layer, half, 0, 0))
    weight_specs = [pl.BlockSpec(memory_space=pl.ANY)] * 3
    gains3 = gains.reshape(gains.shape[0], 2, 3, d)
    if mixer is None:
        in_specs = [row, gain_spec] + weight_specs
        operands = (x2d, gains3, w1, w3, w2)
    else:
        attn_out, w_o, w_o_layer = mixer
        est += 2 * tm * d * 2 + d * d * 2 + tm * d * 4
        tiles = attn_out.shape[2] // tm
        attn_spec = pl.BlockSpec((None, d, tm), lambda i: (i // tiles, 0, i % tiles))
        in_specs = ([row, attn_spec, _resident((None, d, d), lambda i: (w_o_layer, 0, 0)), gain_spec]
                    + weight_specs)
        operands = (x2d, attn_out, w_o, gains3, w1, w3, w2)
    return pl.pallas_call(
        functools.partial(_ffn_kernel, ff_chunks=ff_chunks, mixer=mixer is not None,
                          layer=layer, half=half),
        out_shape=jax.ShapeDtypeStruct((mt, d), F32),
        grid=(mt // tm,),
        in_specs=in_specs,
        out_specs=row,
        scratch_shapes=[
            pltpu.VMEM((tm, ff), BF16),
            pltpu.VMEM((d, ff), BF16), pltpu.VMEM((d, ff), BF16), pltpu.VMEM((ff, d), BF16),
            pltpu.VMEM((2, up_rows, ff), F32), pltpu.VMEM((2, down_rows, d), F32),
            pltpu.SemaphoreType.DMA((2,)), pltpu.SemaphoreType.DMA((2,)),
        ],
        compiler_params=pltpu.CompilerParams(
            dimension_semantics=("arbitrary",), vmem_limit_bytes=_vmem_limit(est)),
        name=f"ffn_l{layer}_h{half}",
    )(*operands)


def _rope_group(x, cos_t, sin_t):
    return x * cos_t + pltpu.roll(x, LANES // 2, 1) * sin_t


def _moba_qkv_kernel(x_ref, g_ref, w_ref, cos_ref, sin_ref, q_ref, k_ref, vt_ref):
    d = D_MODEL
    h = _rms(x_ref[0], g_ref[...]).astype(BF16)
    cos_t = cos_ref[...]
    sin_t = sin_ref[...]
    q = jnp.dot(h, w_ref[:, 0:d], preferred_element_type=F32)
    k = jnp.dot(h, w_ref[:, d:2 * d], preferred_element_type=F32)
    v = jnp.dot(h, w_ref[:, 2 * d:3 * d], preferred_element_type=F32)
    for hd in range(N_HEADS):
        sl = slice(hd * HEAD_DIM, (hd + 1) * HEAD_DIM)
        q_ref[0, :, sl] = _rope_group(q[:, sl], cos_t, sin_t).astype(BF16)
        k_ref[0, :, sl] = _rope_group(k[:, sl], cos_t, sin_t).astype(BF16)
    vt_ref[0] = v.T.astype(BF16)


def _moba_qkv(x3d, gain, w_qkv, layer, cos_t, sin_t, *, tm=1024):
    b, s, d = x3d.shape
    est = 2 * tm * d * 4 + d * 3 * d * 2 + 4 * tm * LANES * 4 + 6 * tm * d * 2 + 4 * tm * d * 4
    return pl.pallas_call(
        _moba_qkv_kernel,
        out_shape=(jax.ShapeDtypeStruct((b, s, d), BF16),
                   jax.ShapeDtypeStruct((b, s, d), BF16),
                   jax.ShapeDtypeStruct((b, d, s), BF16)),
        grid=(b, s // tm),
        in_specs=[
            pl.BlockSpec((1, tm, d), lambda i, j: (i, j, 0)),
            _resident((1, d), lambda i, j: (0, 0)),
            _resident((None, d, 3 * d), lambda i, j: (layer, 0, 0)),
            pl.BlockSpec((tm, LANES), lambda i, j: (j, 0)),
            pl.BlockSpec((tm, LANES), lambda i, j: (j, 0)),
        ],
        out_specs=(pl.BlockSpec((1, tm, d), lambda i, j: (i, j, 0)),
                   pl.BlockSpec((1, tm, d), lambda i, j: (i, j, 0)),
                   pl.BlockSpec((1, d, tm), lambda i, j: (i, 0, j))),
        compiler_params=pltpu.CompilerParams(
            dimension_semantics=("parallel", "parallel"), vmem_limit_bytes=_vmem_limit(est)),
        name="moba_qkv",
    )(x3d, gain.reshape(1, d), w_qkv, cos_t, sin_t)


def _mla_kv_kernel(x_ref, gin_ref, wd_ref, gkv_ref, wu_ref, cos_ref, sin_ref, ka_ref, vt_ref):
    h = _rms(x_ref[0], gin_ref[...]).astype(BF16)
    ckr = jnp.dot(h, wd_ref[...], preferred_element_type=F32)
    c_kv = _rms(ckr[:, :KV_LORA], gkv_ref[...]).astype(BF16)
    k_rope = _rope_group(ckr[:, KV_LORA:], cos_ref[...], sin_ref[...]).astype(BF16)
    kv = jnp.dot(c_kv, wu_ref[...], preferred_element_type=F32)
    for hd in range(N_HEADS):
        base = hd * MLA_QK_PAD
        ka_ref[0, :, base:base + HEAD_DIM] = kv[:, base:base + HEAD_DIM].astype(BF16)
        ka_ref[0, :, base + HEAD_DIM:base + MLA_QK_PAD] = k_rope
        vt_ref[0, hd * HEAD_DIM:(hd + 1) * HEAD_DIM, :] = (
            kv[:, base + HEAD_DIM:base + MLA_QK_PAD].T.astype(BF16))


def _mla_kv(x3d, kv_in_norm, wd_pad, kv_norm, w_ukv, cos_t, sin_t, *, tm=1024):
    b, s, d = x3d.shape
    nd = wd_pad.shape[1]
    nu = w_ukv.shape[1]
    est = 2 * tm * d * 4 + d * nd * 2 + KV_LORA * nu * 2 + 2 * tm * nu * 2 + 2 * tm * d * 2 + 3 * tm * nu * 4
    return pl.pallas_call(
        _mla_kv_kernel,
        out_shape=(jax.ShapeDtypeStruct((b, s, N_HEADS * MLA_QK_PAD), BF16),
                   jax.ShapeDtypeStruct((b, d, s), BF16)),
        grid=(b, s // tm),
        in_specs=[
            pl.BlockSpec((1, tm, d), lambda i, j: (i, j, 0)),
            _resident((1, d), lambda i, j: (0, 0)),
            _resident((d, nd), lambda i, j: (0, 0)),
            _resident((1, KV_LORA), lambda i, j: (0, 0)),
            _resident((KV_LORA, nu), lambda i, j: (0, 0)),
            pl.BlockSpec((tm, LANES), lambda i, j: (j, 0)),
            pl.BlockSpec((tm, LANES), lambda i, j: (j, 0)),
        ],
        out_specs=(pl.BlockSpec((1, tm, N_HEADS * MLA_QK_PAD), lambda i, j: (i, j, 0)),
                   pl.BlockSpec((1, d, tm), lambda i, j: (i, 0, j))),
        compiler_params=pltpu.CompilerParams(
            dimension_semantics=("parallel", "parallel"), vmem_limit_bytes=_vmem_limit(est)),
        name="mla_shared_kv",
    )(x3d, kv_in_norm.reshape(1, d), wd_pad, kv_norm.reshape(1, KV_LORA), w_ukv, cos_t, sin_t)


def _mla_q_kernel(x_ref, g_ref, wd_ref, gq_ref, wu_ref, cos_ref, sin_ref, qa_ref):
    h = _rms(x_ref[0], g_ref[...]).astype(BF16)
    cq = jnp.dot(h, wd_ref[...], preferred_element_type=F32)
    cqn = _rms(cq, gq_ref[...]).astype(BF16)
    qa = jnp.dot(cqn, wu_ref[...], preferred_element_type=F32)
    cos_t = cos_ref[...]
    sin_t = sin_ref[...]
    for hd in range(N_HEADS):
        base = hd * MLA_QK_PAD
        qa_ref[0, :, base:base + HEAD_DIM] = qa[:, base:base + HEAD_DIM].astype(BF16)
        qa_ref[0, :, base + HEAD_DIM:base + MLA_QK_PAD] = _rope_group(
            qa[:, base + HEAD_DIM:base + MLA_QK_PAD], cos_t, sin_t).astype(BF16)


def _mla_q(x3d, gain, w_dq, q_norm, wu_pad, layer, cos_t, sin_t, *, tm=1024):
    b, s, d = x3d.shape
    nu = wu_pad.shape[-1]
    est = 2 * tm * d * 4 + d * Q_LORA * 2 + Q_LORA * nu * 2 + 2 * tm * nu * 2 + 3 * tm * nu * 4
    return pl.pallas_call(
        _mla_q_kernel,
        out_shape=jax.ShapeDtypeStruct((b, s, nu), BF16),
        grid=(b, s // tm),
        in_specs=[
            pl.BlockSpec((1, tm, d), lambda i, j: (i, j, 0)),
            _resident((1, d), lambda i, j: (0, 0)),
            _resident((None, d, Q_LORA), lambda i, j: (layer, 0, 0)),
            _resident((1, Q_LORA), lambda i, j: (0, 0)),
            _resident((None, Q_LORA, nu), lambda i, j: (layer, 0, 0)),
            pl.BlockSpec((tm, LANES), lambda i, j: (j, 0)),
            pl.BlockSpec((tm, LANES), lambda i, j: (j, 0)),
        ],
        out_specs=pl.BlockSpec((1, tm, nu), lambda i, j: (i, j, 0)),
        compiler_params=pltpu.CompilerParams(
            dimension_semantics=("parallel", "parallel"), vmem_limit_bytes=_vmem_limit(est)),
        name="mla_q",
    )(x3d, gain.reshape(1, d), w_dq, q_norm.reshape(1, Q_LORA), wu_pad, cos_t, sin_t)


def _moba_keep(gate, own):
    rows = [gate[n:n + 1, :] for n in range(own)]
    keep = []
    for n in range(own):
        rank = jnp.zeros_like(rows[n])
        for m in range(own):
            if m < n:
                rank = rank + jnp.where(rows[m] >= rows[n], 1.0, 0.0)
            elif m > n:
                rank = rank + jnp.where(rows[m] > rows[n], 1.0, 0.0)
        keep.append(rank < float(MOBA_TOPK))
    return keep


_NT = (((1,), (1,)), ((), ()))


class _Head:
    def __init__(self, q_ref, k_ref, vt_ref, o_ref, idx, dk, dv, moba):
        self.q_ref, self.k_ref, self.vt_ref, self.o_ref = q_ref, k_ref, vt_ref, o_ref
        self.cols = slice(idx * dk, (idx + 1) * dk)
        self.rows = slice(idx * dv, (idx + 1) * dv)
        self.dv, self.moba = dv, moba
        self.blk = MOBA_BLOCK
        self.nb = q_ref.shape[1] // self.blk
        self.ones_rows = jnp.ones((SUM_ROWS, q_ref.shape[1]), BF16)
        if moba:
            sub_i = lax.broadcasted_iota(jnp.int32, (8, dk), 0)
            km = jnp.zeros((8, dk), F32)
            for n in range(self.nb - 1):
                row = jnp.sum(self.k(n).astype(F32), axis=0, keepdims=True)
                km = jnp.where(sub_i == n, row * (1.0 / self.blk), km)
            km_hi = km.astype(BF16)
            km_lo = (km - km_hi.astype(F32)).astype(BF16)
            self.km_hl = jnp.concatenate([km_hi, km_lo], axis=0)

    def q(self, qi):
        return self.q_ref[0, qi * self.blk:(qi + 1) * self.blk, self.cols]

    def k(self, n):
        return self.k_ref[0, n * self.blk:(n + 1) * self.blk, self.cols]

    def k_span(self, n0, count):
        return self.k_ref[0, n0 * self.blk:(n0 + count) * self.blk, self.cols]

    def keep_rows(self, qi, q_blk):
        if not (self.moba and qi > MOBA_TOPK):
            return None
        g2 = lax.dot_general(self.km_hl, q_blk, _NT, preferred_element_type=F32)
        return _moba_keep(g2[0:8, :] + g2[8:16, :], qi)

    def values(self, qi, p_list):
        p_t = p_list[0] if len(p_list) == 1 else jnp.concatenate(p_list, axis=0)
        kv = (qi + 1) * self.blk
        vt_aug = jnp.concatenate([self.vt_ref[0, self.rows, 0:kv], self.ones_rows[:, 0:kv]], axis=0)
        out_aug = jnp.dot(vt_aug, p_t, preferred_element_type=F32)
        denom = out_aug[self.dv:self.dv + 1, :]
        out_t = out_aug[0:self.dv, :] * (1.0 / denom)
        self.o_ref[0, self.rows, qi * self.blk:(qi + 1) * self.blk] = out_t.astype(BF16)
        return denom


def _attn_kernel(q_ref, k_ref, vt_ref, o_ref, *, moba, exp_scale, heads):
    blk = MOBA_BLOCK
    dk = q_ref.shape[2] // heads
    dv = vt_ref.shape[1] // heads
    key_i = lax.broadcasted_iota(jnp.int32, (blk, blk), 0)
    qry_i = lax.broadcasted_iota(jnp.int32, (blk, blk), 1)
    causal_bias = jnp.where(key_i <= qry_i, 0.0, -jnp.inf)
    head_views = [_Head(q_ref, k_ref, vt_ref, o_ref, i, dk, dv, moba) for i in range(heads)]

    lo, hi = 2.0 ** -DENOM_LOG2_RANGE, 2.0 ** DENOM_LOG2_RANGE
    suspect = None
    for hv in head_views:
        for qi in reversed(range(hv.nb)):
            q_blk = hv.q(qi)
            keep = hv.keep_rows(qi, q_blk)
            p_list = []
            t_blocks = []
            for n0 in range(0, qi + 1, QK_BLOCKS_PER_DOT):
                cnt = min(QK_BLOCKS_PER_DOT, qi + 1 - n0)
                t_grp = lax.dot_general(hv.k_span(n0, cnt), q_blk, _NT,
                                        preferred_element_type=F32) * exp_scale
                t_blocks += [t_grp[i * blk:(i + 1) * blk, :] for i in range(cnt)]
            for n in range(qi + 1):
                t = t_blocks[n]
                if n == qi:
                    t = t + causal_bias
                p = jnp.exp2(t).astype(BF16)
                if n < qi and keep is not None:
                    p = p * jnp.where(keep[n], 1.0, 0.0).astype(BF16)
                p_list.append(p)
            denom = hv.values(qi, p_list)
            flag = jnp.where(jnp.logical_and(denom >= lo, denom <= hi), 0.0, 1.0)
            suspect = flag if suspect is None else jnp.maximum(suspect, flag)

    @pl.when(jnp.max(suspect) > 0.0)
    def _():
        for hv in head_views:
            _shifted_softmax_pass(hv, causal_bias, exp_scale)


def _shifted_softmax_pass(hv, causal_bias, exp_scale):
    nb, blk = hv.nb, hv.blk

    def logits_begin(qi):
        q_blk = hv.q(qi)
        keep = hv.keep_rows(qi, q_blk)
        sel_bias = None if keep is None else [jnp.where(kp, 0.0, -jnp.inf) for kp in keep]
        return dict(qi=qi, q=q_blk, bias=sel_bias, s=[], m8=None)

    def logits_step(st, n):
        qi = st["qi"]
        s = lax.dot_general(hv.k(n), st["q"], _NT, preferred_element_type=F32)
        if n == qi:
            s = s + causal_bias
        st["s"].append(s)
        bm8 = jnp.max(s.reshape(blk // 8, 8, blk), axis=0)
        if n < qi and st["bias"] is not None:
            bm8 = bm8 + st["bias"][n]
        st["m8"] = bm8 if st["m8"] is None else jnp.maximum(st["m8"], bm8)

    def weights_step(st, m, n):
        m_n = m - st["bias"][n] if (n < st["qi"] and st["bias"] is not None) else m
        return jnp.exp2((st["s"][n] - m_n) * exp_scale).astype(BF16)

    def logits_all(qi):
        st = logits_begin(qi)
        for n in range(qi + 1):
            logits_step(st, n)
        return st

    ahead = [logits_all(qi) for qi in range(min(ATTN_LOOKAHEAD, nb))]
    for qi in range(nb):
        if qi + ATTN_LOOKAHEAD < nb:
            ahead.append(logits_all(qi + ATTN_LOOKAHEAD))
        cur = ahead.pop(0)
        m = jnp.max(cur["m8"], axis=0, keepdims=True)
        hv.values(qi, [weights_step(cur, m, n) for n in range(qi + 1)])


def _attention(q, k, vt, *, moba, qk_dim, scale, heads=ATTN_HEADS_PER_STEP):
    b, s, _ = q.shape
    dv = HEAD_DIM
    est = heads * (2 * (2 * s * qk_dim * 2 + 2 * s * dv * 2) + 24 * MOBA_BLOCK * s * 4)
    return pl.pallas_call(
        functools.partial(_attn_kernel, moba=moba, exp_scale=scale * LOG2_E, heads=heads),
        out_shape=jax.ShapeDtypeStruct((b, N_HEADS * dv, s), BF16),
        grid=(b, N_HEADS // heads),
        in_specs=[
            pl.BlockSpec((1, s, heads * qk_dim), lambda i, j: (i, 0, j)),
            pl.BlockSpec((1, s, heads * qk_dim), lambda i, j: (i, 0, j)),
            pl.BlockSpec((1, heads * dv, s), lambda i, j: (i, j, 0)),
        ],
        out_specs=pl.BlockSpec((1, heads * dv, s), lambda i, j: (i, j, 0)),
        compiler_params=pltpu.CompilerParams(
            dimension_semantics=("parallel", "parallel"), vmem_limit_bytes=_vmem_limit(est)),
        name="moba_attn" if moba else "mla_attn",
    )(q, k, vt)


def _rope_tables(seq_len, dim):
    inv_freq = 1.0 / (ROPE_THETA ** (jnp.arange(0, dim, 2, dtype=F32) / dim))
    ang = jnp.arange(seq_len, dtype=F32)[:, None] * inv_freq[None, :]
    return jnp.cos(ang), jnp.sin(ang)


def _rope_group_layout(main, rope):
    half = ROPE_DIM // 2
    zeros = jnp.zeros(rope.shape[:-1] + (LANES // 2 - half,), rope.dtype)
    return jnp.concatenate([main, rope[..., :half], zeros, rope[..., half:], zeros], axis=-1)


def kernel(x, norm_gains, ffn_w1, ffn_w3, ffn_w2, moba_w_qkv, moba_w_o, mla_w_dq, mla_q_norm,
           mla_w_uq, mla_w_o, kv_in_norm, w_dkv_kr, kv_norm, w_ukv):
    b, s, d = x.shape
    depth = norm_gains.shape[0]
    n_a = moba_w_qkv.shape[0]

    cos_a, sin_a = _rope_tables(s, HEAD_DIM)
    cos_moba = jnp.concatenate([cos_a, cos_a], axis=1)
    sin_moba = jnp.concatenate([-sin_a, sin_a], axis=1)
    cos_r, sin_r = _rope_tables(s, ROPE_DIM)
    zpad = jnp.zeros_like(cos_r)
    cos_mla = jnp.concatenate([cos_r, zpad, cos_r, zpad], axis=1)
    sin_mla = jnp.concatenate([-sin_r, zpad, sin_r, zpad], axis=1)

    w1, w3, w2 = ffn_w1, ffn_w3, ffn_w2
    w_qkv = moba_w_qkv.astype(BF16)
    w_o_a = moba_w_o.astype(BF16)
    w_o_b = mla_w_o.astype(BF16)
    w_dq = mla_w_dq.astype(BF16)
    wd_pad = _rope_group_layout(w_dkv_kr[:, :KV_LORA], w_dkv_kr[:, KV_LORA:]).astype(BF16)
    uq = mla_w_uq.reshape(mla_w_uq.shape[0], Q_LORA, N_HEADS, HEAD_DIM + ROPE_DIM)
    wu_pad = _rope_group_layout(uq[..., :HEAD_DIM], uq[..., HEAD_DIM:])
    wu_pad = wu_pad.reshape(mla_w_uq.shape[0], Q_LORA, N_HEADS * MLA_QK_PAD).astype(BF16)
    w_ukv_b = w_ukv.astype(BF16)

    x2 = x.reshape(b * s, d)
    k_aug = vt_shared = None
    for l in range(depth):
        g = norm_gains[l]
        x2 = _ffn(x2, norm_gains, w1, w3, w2, l, 0)
        x3 = x2.reshape(b, s, d)
        if l < n_a:
            q, k, vt = _moba_qkv(x3, g[2], w_qkv, l, cos_moba, sin_moba)
            o = _attention(q, k, vt, moba=True, qk_dim=HEAD_DIM, scale=HEAD_DIM ** -0.5)
            mixer = (o, w_o_a, l)
        else:
            j = l - n_a
            q_aug = _mla_q(x3, g[2], w_dq, mla_q_norm[j], wu_pad, j, cos_mla, sin_mla)
            o = _attention(q_aug, k_aug, vt_shared, moba=False, qk_dim=MLA_QK_PAD,
                           scale=(HEAD_DIM + ROPE_DIM) ** -0.5)
            mixer = (o, w_o_b, j)
        x2 = _ffn(x2, norm_gains, w1, w3, w2, l, 1, mixer)
        if l == n_a - 1:
            k_aug, vt_shared = _mla_kv(x2.reshape(b, s, d), kv_in_norm, wd_pad, kv_norm,
                                       w_ukv_b, cos_mla, sin_mla)
    return x2.reshape(b, s, d)
```

```python
import functools

import jax
import jax.numpy as jnp
from jax import lax
from jax.experimental import pallas as pl
from jax.experimental.pallas import tpu as pltpu

D_MODEL = 1024
N_HEADS = 8
HEAD_DIM = 128
MOBA_BLOCK = 256
MOBA_TOPK = 3
ROPE_DIM = 64
KV_LORA = 256
Q_LORA = 512
ROPE_THETA = 10000.0
NORM_EPS = 1e-6
LANES = 128
MLA_QK_PAD = 2 * LANES
SUM_ROWS = 16
MIXER_ROW_GROUPS = 2
QK_BLOCKS_PER_DOT = 2
ATTN_HEADS_PER_STEP = 4
ATTN_LOOKAHEAD = 3
DENOM_LOG2_RANGE = 60.0
V7X_VMEM_BYTES = 64 * 1024 * 1024
LOG2_E = 1.4426950408889634

F32 = jnp.float32
BF16 = jnp.bfloat16


def _vmem_limit(estimate_bytes):
    return int(min(estimate_bytes * 3 // 2, V7X_VMEM_BYTES * 15 // 16))


def _rms(x, g):
    ms = jnp.mean(x * x, axis=-1, keepdims=True)
    return x * lax.rsqrt(ms + NORM_EPS) * g


def _resident(shape, index_map):
    return pl.BlockSpec(shape, index_map, pipeline_mode=pl.Buffered(1))


def _swiglu_up(h_of, w1_ref, w3_ref, s_ref, rows, ff_chunks):
    for start, size in ff_chunks:
        a = jnp.dot(h_of(), w1_ref[:, start:start + size], preferred_element_type=F32)
        b = jnp.dot(h_of(), w3_ref[:, start:start + size], preferred_element_type=F32)
        silu = a * (1.0 / (1.0 + jnp.exp(-a)))
        s_ref[rows, start:start + size] = (silu * b).astype(BF16)


def _ffn_plain_body(x_ref, g_ref, w1_ref, w3_ref, w2_ref, o_ref, s_ref, *, ff_chunks):
    x = x_ref[...]
    h = _rms(x, g_ref[0:1, :]).astype(BF16)
    _swiglu_up(lambda: h, w1_ref, w3_ref, s_ref, slice(None), ff_chunks)
    y = jnp.dot(s_ref[...], w2_ref[...], preferred_element_type=F32)
    o_ref[...] = x + 0.5 * _rms(y, g_ref[1:2, :])


def _ffn_mixer_body(x_ref, a_ref, wo_ref, g_ref, w1_ref, w3_ref, w2_ref, o_ref, s_ref, *, ff_chunks):
    tm = x_ref.shape[0]
    groups = [slice(r, r + tm // MIXER_ROW_GROUPS) for r in range(0, tm, tm // MIXER_ROW_GROUPS)]
    xs, hs = [], []
    for rows in groups:
        mix = lax.dot_general(a_ref[:, rows], wo_ref[...], (((0,), (0,)), ((), ())),
                              preferred_element_type=F32)
        x = x_ref[rows, :] + _rms(mix, g_ref[0:1, :])
        xs.append(x)
        hs.append(_rms(x, g_ref[1:2, :]).astype(BF16))
    for rows, h in zip(groups, hs):
        _swiglu_up(lambda h=h: h, w1_ref, w3_ref, s_ref, rows, ff_chunks)
    for rows, x in zip(groups, xs):
        y = jnp.dot(s_ref[rows, :], w2_ref[...], preferred_element_type=F32)
        o_ref[rows, :] = x + 0.5 * _rms(y, g_ref[2:3, :])


def _ffn(x2d, gains, w1, w3, w2, layer, half, mixer=None, *, tm=1024):
    mt, d = x2d.shape
    ff = w1.shape[-1]
    chunk = 4 * LANES
    ff_chunks = tuple((s, min(chunk, ff - s)) for s in range(0, ff, chunk))
    est = (4 * tm * d * 4 + 3 * d * ff * 2 + tm * ff * 2 + 4 * tm * chunk * 4 + 2 * tm * d * 4)
    row = pl.BlockSpec((tm, d), lambda i: (i, 0))
    gain_spec = _resident((None, None, 3, d), lambda i: (layer, half, 0, 0))
    weight_specs = [
        _resident((None, None, d, ff), lambda i: (layer, half, 0, 0)),
        _resident((None, None, d, ff), lambda i: (layer, half, 0, 0)),
        _resident((None, None, ff, d), lambda i: (layer, half, 0, 0)),
    ]
    gains3 = gains.reshape(gains.shape[0], 2, 3, d)
    if mixer is None:
        body = _ffn_plain_body
        in_specs = [row, gain_spec] + weight_specs
        operands = (x2d, gains3, w1, w3, w2)
    else:
        body = _ffn_mixer_body
        attn_out, w_o, w_o_layer = mixer
        est += 2 * tm * d * 2 + d * d * 2 + tm * d * 4
        tiles = attn_out.shape[2] // tm
        attn_spec = pl.BlockSpec((None, d, tm), lambda i: (i // tiles, 0, i % tiles))
        in_specs = ([row, attn_spec, _resident((None, d, d), lambda i: (w_o_layer, 0, 0)), gain_spec]
                    + weight_specs)
        operands = (x2d, attn_out, w_o, gains3, w1, w3, w2)
    return pl.pallas_call(
        functools.partial(body, ff_chunks=ff_chunks),
        out_shape=jax.ShapeDtypeStruct((mt, d), F32),
        grid=(mt // tm,),
        in_specs=in_specs,
        out_specs=row,
        scratch_shapes=[pltpu.VMEM((tm, ff), BF16)],
        compiler_params=pltpu.CompilerParams(
            dimension_semantics=("parallel",), vmem_limit_bytes=_vmem_limit(est)),
        name=f"ffn_l{layer}_h{half}",
    )(*operands)


def _rope_group(x, cos_t, sin_t):
    return x * cos_t + pltpu.roll(x, LANES // 2, 1) * sin_t


def _moba_qkv_kernel(x_ref, g_ref, w_ref, cos_ref, sin_ref, q_ref, k_ref, vt_ref):
    d = D_MODEL
    h = _rms(x_ref[0], g_ref[...]).astype(BF16)
    cos_t = cos_ref[...]
    sin_t = sin_ref[...]
    q = jnp.dot(h, w_ref[:, 0:d], preferred_element_type=F32)
    k = jnp.dot(h, w_ref[:, d:2 * d], preferred_element_type=F32)
    v = jnp.dot(h, w_ref[:, 2 * d:3 * d], preferred_element_type=F32)
    for hd in range(N_HEADS):
        sl = slice(hd * HEAD_DIM, (hd + 1) * HEAD_DIM)
        q_ref[0, :, sl] = _rope_group(q[:, sl], cos_t, sin_t).astype(BF16)
        k_ref[0, :, sl] = _rope_group(k[:, sl], cos_t, sin_t).astype(BF16)
    vt_ref[0] = v.T.astype(BF16)


def _moba_qkv(x3d, gain, w_qkv, layer, cos_t, sin_t, *, tm=1024):
    b, s, d = x3d.shape
    est = 2 * tm * d * 4 + d * 3 * d * 2 + 4 * tm * LANES * 4 + 6 * tm * d * 2 + 4 * tm * d * 4
    return pl.pallas_call(
        _moba_qkv_kernel,
        out_shape=(jax.ShapeDtypeStruct((b, s, d), BF16),
                   jax.ShapeDtypeStruct((b, s, d), BF16),
                   jax.ShapeDtypeStruct((b, d, s), BF16)),
        grid=(b, s // tm),
        in_specs=[
            pl.BlockSpec((1, tm, d), lambda i, j: (i, j, 0)),
            _resident((1, d), lambda i, j: (0, 0)),
            _resident((None, d, 3 * d), lambda i, j: (layer, 0, 0)),
            pl.BlockSpec((tm, LANES), lambda i, j: (j, 0)),
            pl.BlockSpec((tm, LANES), lambda i, j: (j, 0)),
        ],
        out_specs=(pl.BlockSpec((1, tm, d), lambda i, j: (i, j, 0)),
                   pl.BlockSpec((1, tm, d), lambda i, j: (i, j, 0)),
                   pl.BlockSpec((1, d, tm), lambda i, j: (i, 0, j))),
        compiler_params=pltpu.CompilerParams(
            dimension_semantics=("parallel", "parallel"), vmem_limit_bytes=_vmem_limit(est)),
        name="moba_qkv",
    )(x3d, gain.reshape(1, d), w_qkv, cos_t, sin_t)


def _mla_kv_kernel(x_ref, gin_ref, wd_ref, gkv_ref, wu_ref, cos_ref, sin_ref, ka_ref, vt_ref):
    h = _rms(x_ref[0], gin_ref[...]).astype(BF16)
    ckr = jnp.dot(h, wd_ref[...], preferred_element_type=F32)
    c_kv = _rms(ckr[:, :KV_LORA], gkv_ref[...]).astype(BF16)
    k_rope = _rope_group(ckr[:, KV_LORA:], cos_ref[...], sin_ref[...]).astype(BF16)
    kv = jnp.dot(c_kv, wu_ref[...], preferred_element_type=F32)
    for hd in range(N_HEADS):
        base = hd * MLA_QK_PAD
        ka_ref[0, :, base:base + HEAD_DIM] = kv[:, base:base + HEAD_DIM].astype(BF16)
        ka_ref[0, :, base + HEAD_DIM:base + MLA_QK_PAD] = k_rope
        vt_ref[0, hd * HEAD_DIM:(hd + 1) * HEAD_DIM, :] = (
            kv[:, base + HEAD_DIM:base + MLA_QK_PAD].T.astype(BF16))


def _mla_kv(x3d, kv_in_norm, wd_pad, kv_norm, w_ukv, cos_t, sin_t, *, tm=1024):
    b, s, d = x3d.shape
    nd = wd_pad.shape[1]
    nu = w_ukv.shape[1]
    est = 2 * tm * d * 4 + d * nd * 2 + KV_LORA * nu * 2 + 2 * tm * nu * 2 + 2 * tm * d * 2 + 3 * tm * nu * 4
    return pl.pallas_call(
        _mla_kv_kernel,
        out_shape=(jax.ShapeDtypeStruct((b, s, N_HEADS * MLA_QK_PAD), BF16),
                   jax.ShapeDtypeStruct((b, d, s), BF16)),
        grid=(b, s // tm),
        in_specs=[
            pl.BlockSpec((1, tm, d), lambda i, j: (i, j, 0)),
            _resident((1, d), lambda i, j: (0, 0)),
            _resident((d, nd), lambda i, j: (0, 0)),
            _resident((1, KV_LORA), lambda i, j: (0, 0)),
            _resident((KV_LORA, nu), lambda i, j: (0, 0)),
            pl.BlockSpec((tm, LANES), lambda i, j: (j, 0)),
            pl.BlockSpec((tm, LANES), lambda i, j: (j, 0)),
        ],
        out_specs=(pl.BlockSpec((1, tm, N_HEADS * MLA_QK_PAD), lambda i, j: (i, j, 0)),
                   pl.BlockSpec((1, d, tm), lambda i, j: (i, 0, j))),
        compiler_params=pltpu.CompilerParams(
            dimension_semantics=("parallel", "parallel"), vmem_limit_bytes=_vmem_limit(est)),
        name="mla_shared_kv",
    )(x3d, kv_in_norm.reshape(1, d), wd_pad, kv_norm.reshape(1, KV_LORA), w_ukv, cos_t, sin_t)


def _mla_q_kernel(x_ref, g_ref, wd_ref, gq_ref, wu_ref, cos_ref, sin_ref, qa_ref):
    h = _rms(x_ref[0], g_ref[...]).astype(BF16)
    cq = jnp.dot(h, wd_ref[...], preferred_element_type=F32)
    cqn = _rms(cq, gq_ref[...]).astype(BF16)
    qa = jnp.dot(cqn, wu_ref[...], preferred_element_type=F32)
    cos_t = cos_ref[...]
    sin_t = sin_ref[...]
    for hd in range(N_HEADS):
        base = hd * MLA_QK_PAD
        qa_ref[0, :, base:base + HEAD_DIM] = qa[:, base:base + HEAD_DIM].astype(BF16)
        qa_ref[0, :, base + HEAD_DIM:base + MLA_QK_PAD] = _rope_group(
            qa[:, base + HEAD_DIM:base + MLA_QK_PAD], cos_t, sin_t).astype(BF16)


def _mla_q(x3d, gain, w_dq, q_norm, wu_pad, layer, cos_t, sin_t, *, tm=1024):
    b, s, d = x3d.shape
    nu = wu_pad.shape[-1]
    est = 2 * tm * d * 4 + d * Q_LORA * 2 + Q_LORA * nu * 2 + 2 * tm * nu * 2 + 3 * tm * nu * 4
    return pl.pallas_call(
        _mla_q_kernel,
        out_shape=jax.ShapeDtypeStruct((b, s, nu), BF16),
        grid=(b, s // tm),
        in_specs=[
            pl.BlockSpec((1, tm, d), lambda i, j: (i, j, 0)),
            _resident((1, d), lambda i, j: (0, 0)),
            _resident((None, d, Q_LORA), lambda i, j: (layer, 0, 0)),
            _resident((1, Q_LORA), lambda i, j: (0, 0)),
            _resident((None, Q_LORA, nu), lambda i, j: (layer, 0, 0)),
            pl.BlockSpec((tm, LANES), lambda i, j: (j, 0)),
            pl.BlockSpec((tm, LANES), lambda i, j: (j, 0)),
        ],
        out_specs=pl.BlockSpec((1, tm, nu), lambda i, j: (i, j, 0)),
        compiler_params=pltpu.CompilerParams(
            dimension_semantics=("parallel", "parallel"), vmem_limit_bytes=_vmem_limit(est)),
        name="mla_q",
    )(x3d, gain.reshape(1, d), w_dq, q_norm.reshape(1, Q_LORA), wu_pad, cos_t, sin_t)


def _moba_keep(gate, own):
    rows = [gate[n:n + 1, :] for n in range(own)]
    keep = []
    for n in range(own):
        rank = jnp.zeros_like(rows[n])
        for m in range(own):
            if m < n:
                rank = rank + jnp.where(rows[m] >= rows[n], 1.0, 0.0)
            elif m > n:
                rank = rank + jnp.where(rows[m] > rows[n], 1.0, 0.0)
        keep.append(rank < float(MOBA_TOPK))
    return keep


_NT = (((1,), (1,)), ((), ()))


class _Head:
    def __init__(self, q_ref, k_ref, vt_ref, o_ref, idx, dk, dv, moba):
        self.q_ref, self.k_ref, self.vt_ref, self.o_ref = q_ref, k_ref, vt_ref, o_ref
        self.cols = slice(idx * dk, (idx + 1) * dk)
        self.rows = slice(idx * dv, (idx + 1) * dv)
        self.dv, self.moba = dv, moba
        self.blk = MOBA_BLOCK
        self.nb = q_ref.shape[1] // self.blk
        self.ones_rows = jnp.ones((SUM_ROWS, q_ref.shape[1]), BF16)
        if moba:
            sub_i = lax.broadcasted_iota(jnp.int32, (8, dk), 0)
            km = jnp.zeros((8, dk), F32)
            for n in range(self.nb - 1):
                row = jnp.sum(self.k(n).astype(F32), axis=0, keepdims=True)
                km = jnp.where(sub_i == n, row * (1.0 / self.blk), km)
            km_hi = km.astype(BF16)
            km_lo = (km - km_hi.astype(F32)).astype(BF16)
            self.km_hl = jnp.concatenate([km_hi, km_lo], axis=0)

    def q(self, qi):
        return self.q_ref[0, qi * self.blk:(qi + 1) * self.blk, self.cols]

    def k(self, n):
        return self.k_ref[0, n * self.blk:(n + 1) * self.blk, self.cols]

    def k_span(self, n0, count):
        return self.k_ref[0, n0 * self.blk:(n0 + count) * self.blk, self.cols]

    def keep_rows(self, qi, q_blk):
        if not (self.moba and qi > MOBA_TOPK):
            return None
        g2 = lax.dot_general(self.km_hl, q_blk, _NT, preferred_element_type=F32)
        return _moba_keep(g2[0:8, :] + g2[8:16, :], qi)

    def values(self, qi, p_list):
        p_t = p_list[0] if len(p_list) == 1 else jnp.concatenate(p_list, axis=0)
        kv = (qi + 1) * self.blk
        vt_aug = jnp.concatenate([self.vt_ref[0, self.rows, 0:kv], self.ones_rows[:, 0:kv]], axis=0)
        out_aug = jnp.dot(vt_aug, p_t, preferred_element_type=F32)
        denom = out_aug[self.dv:self.dv + 1, :]
        out_t = out_aug[0:self.dv, :] * (1.0 / denom)
        self.o_ref[0, self.rows, qi * self.blk:(qi + 1) * self.blk] = out_t.astype(BF16)
        return denom


def _attn_kernel(q_ref, k_ref, vt_ref, o_ref, *, moba, exp_scale, heads):
    blk = MOBA_BLOCK
    dk = q_ref.shape[2] // heads
    dv = vt_ref.shape[1] // heads
    key_i = lax.broadcasted_iota(jnp.int32, (blk, blk), 0)
    qry_i = lax.broadcasted_iota(jnp.int32, (blk, blk), 1)
    causal_bias = jnp.where(key_i <= qry_i, 0.0, -jnp.inf)
    head_views = [_Head(q_ref, k_ref, vt_ref, o_ref, i, dk, dv, moba) for i in range(heads)]

    lo, hi = 2.0 ** -DENOM_LOG2_RANGE, 2.0 ** DENOM_LOG2_RANGE
    def weights(hv, qi):
        q_blk = hv.q(qi)
        keep = hv.keep_rows(qi, q_blk)
        p_list = []
        t_blocks = []
        for n0 in range(0, qi + 1, QK_BLOCKS_PER_DOT):
            cnt = min(QK_BLOCKS_PER_DOT, qi + 1 - n0)
            t_grp = lax.dot_general(hv.k_span(n0, cnt), q_blk, _NT,
                                    preferred_element_type=F32) * exp_scale
            t_blocks += [t_grp[i * blk:(i + 1) * blk, :] for i in range(cnt)]
        for n in range(qi + 1):
            t = t_blocks[n]
            if n == qi:
                t = t + causal_bias
            p = jnp.exp2(t).astype(BF16)
            if n < qi and keep is not None:
                p = p * jnp.where(keep[n], 1.0, 0.0).astype(BF16)
            p_list.append(p)
        return p_list

    items = [(hv, qi) for hv in head_views for qi in reversed(range(hv.nb))]
    suspect = None
    pending = weights(*items[0])
    for idx, (hv, qi) in enumerate(items):
        ahead = weights(*items[idx + 1]) if idx + 1 < len(items) else None
        denom = hv.values(qi, pending)
        pending = ahead
        flag = jnp.where(jnp.logical_and(denom >= lo, denom <= hi), 0.0, 1.0)
        suspect = flag if suspect is None else jnp.maximum(suspect, flag)

    @pl.when(jnp.max(suspect) > 0.0)
    def _():
        for hv in head_views:
            _shifted_softmax_pass(hv, causal_bias, exp_scale)


def _shifted_softmax_pass(hv, causal_bias, exp_scale):
    nb, blk = hv.nb, hv.blk

    def logits_begin(qi):
        q_blk = hv.q(qi)
        keep = hv.keep_rows(qi, q_blk)
        sel_bias = None if keep is None else [jnp.where(kp, 0.0, -jnp.inf) for kp in keep]
        return dict(qi=qi, q=q_blk, bias=sel_bias, s=[], m8=None)

    def logits_step(st, n):
        qi = st["qi"]
        s = lax.dot_general(hv.k(n), st["q"], _NT, preferred_element_type=F32)
        if n == qi:
            s = s + causal_bias
        st["s"].append(s)
        bm8 = jnp.max(s.reshape(blk // 8, 8, blk), axis=0)
        if n < qi and st["bias"] is not None:
            bm8 = bm8 + st["bias"][n]
        st["m8"] = bm8 if st["m8"] is None else jnp.maximum(st["m8"], bm8)

    def weights_step(st, m, n):
        m_n = m - st["bias"][n] if (n < st["qi"] and st["bias"] is not None) else m
        return jnp.exp2((st["s"][n] - m_n) * exp_scale).astype(BF16)

    def logits_all(qi):
        st = logits_begin(qi)
        for n in range(qi + 1):
            logits_step(st, n)
        return st

    ahead = [logits_all(qi) for qi in range(min(ATTN_LOOKAHEAD, nb))]
    for qi in range(nb):
        if qi + ATTN_LOOKAHEAD < nb:
            ahead.append(logits_all(qi + ATTN_LOOKAHEAD))
        cur = ahead.pop(0)
        m = jnp.max(cur["m8"], axis=0, keepdims=True)
        hv.values(qi, [weights_step(cur, m, n) for n in range(qi + 1)])


def _attention(q, k, vt, *, moba, qk_dim, scale, heads=ATTN_HEADS_PER_STEP):
    b, s, _ = q.shape
    dv = HEAD_DIM
    est = heads * (2 * (2 * s * qk_dim * 2 + 2 * s * dv * 2) + 24 * MOBA_BLOCK * s * 4)
    return pl.pallas_call(
        functools.partial(_attn_kernel, moba=moba, exp_scale=scale * LOG2_E, heads=heads),
        out_shape=jax.ShapeDtypeStruct((b, N_HEADS * dv, s), BF16),
        grid=(b, N_HEADS // heads),
        in_specs=[
            pl.BlockSpec((1, s, heads * qk_dim), lambda i, j: (i, 0, j)),
            pl.BlockSpec((1, s, heads * qk_dim), lambda i, j: (i, 0, j)),
            pl.BlockSpec((1, heads * dv, s), lambda i, j: (i, j, 0)),
        ],
        out_specs=pl.BlockSpec((1, heads * dv, s), lambda i, j: (i, j, 0)),
        compiler_params=pltpu.CompilerParams(
            dimension_semantics=("parallel", "parallel"), vmem_limit_bytes=_vmem_limit(est)),
        name="moba_attn" if moba else "mla_attn",
    )(q, k, vt)


def _rope_tables(seq_len, dim):
    inv_freq = 1.0 / (ROPE_THETA ** (jnp.arange(0, dim, 2, dtype=F32) / dim))
    ang = jnp.arange(seq_len, dtype=F32)[:, None] * inv_freq[None, :]
    return jnp.cos(ang), jnp.sin(ang)


def _rope_group_layout(main, rope):
    half = ROPE_DIM // 2
    zeros = jnp.zeros(rope.shape[:-1] + (LANES // 2 - half,), rope.dtype)
    return jnp.concatenate([main, rope[..., :half], zeros, rope[..., half:], zeros], axis=-1)


def kernel(x, norm_gains, ffn_w1, ffn_w3, ffn_w2, moba_w_qkv, moba_w_o, mla_w_dq, mla_q_norm,
           mla_w_uq, mla_w_o, kv_in_norm, w_dkv_kr, kv_norm, w_ukv):
    b, s, d = x.shape
    depth = norm_gains.shape[0]
    n_a = moba_w_qkv.shape[0]

    cos_a, sin_a = _rope_tables(s, HEAD_DIM)
    cos_moba = jnp.concatenate([cos_a, cos_a], axis=1)
    sin_moba = jnp.concatenate([-sin_a, sin_a], axis=1)
    cos_r, sin_r = _rope_tables(s, ROPE_DIM)
    zpad = jnp.zeros_like(cos_r)
    cos_mla = jnp.concatenate([cos_r, zpad, cos_r, zpad], axis=1)
    sin_mla = jnp.concatenate([-sin_r, zpad, sin_r, zpad], axis=1)

    w1 = ffn_w1.astype(BF16)
    w3 = ffn_w3.astype(BF16)
    w2 = ffn_w2.astype(BF16)
    w_qkv = moba_w_qkv.astype(BF16)
    w_o_a = moba_w_o.astype(BF16)
    w_o_b = mla_w_o.astype(BF16)
    w_dq = mla_w_dq.astype(BF16)
    wd_pad = _rope_group_layout(w_dkv_kr[:, :KV_LORA], w_dkv_kr[:, KV_LORA:]).astype(BF16)
    uq = mla_w_uq.reshape(mla_w_uq.shape[0], Q_LORA, N_HEADS, HEAD_DIM + ROPE_DIM)
    wu_pad = _rope_group_layout(uq[..., :HEAD_DIM], uq[..., HEAD_DIM:])
    wu_pad = wu_pad.reshape(mla_w_uq.shape[0], Q_LORA, N_HEADS * MLA_QK_PAD).astype(BF16)
    w_ukv_b = w_ukv.astype(BF16)

    x2 = x.reshape(b * s, d)
    k_aug = vt_shared = None
    for l in range(depth):
        g = norm_gains[l]
        x2 = _ffn(x2, norm_gains, w1, w3, w2, l, 0)
        x3 = x2.reshape(b, s, d)
        if l < n_a:
            q, k, vt = _moba_qkv(x3, g[2], w_qkv, l, cos_moba, sin_moba)
            o = _attention(q, k, vt, moba=True, qk_dim=HEAD_DIM, scale=HEAD_DIM ** -0.5)
            mixer = (o, w_o_a, l)
        else:
            j = l - n_a
            q_aug = _mla_q(x3, g[2], w_dq, mla_q_norm[j], wu_pad, j, cos_mla, sin_mla)
            o = _attention(q_aug, k_aug, vt_shared, moba=False, qk_dim=MLA_QK_PAD,
                           scale=(HEAD_DIM + ROPE_DIM) ** -0.5)
            mixer = (o, w_o_b, j)
        x2 = _ffn(x2, norm_gains, w1, w3, w2, l, 1, mixer)
        if l == n_a - 1:
            k_aug, vt_shared = _mla_kv(x2.reshape(b, s, d), kv_in_norm, wd_pad, kv_norm,
                                       w_ukv_b, cos_mla, sin_mla)
    return x2.reshape(b, s, d)
```

```python
import functools

import jax
import jax.numpy as jnp
from jax import lax
from jax.experimental import pallas as pl
from jax.experimental.pallas import tpu as pltpu

D_MODEL = 1024
N_HEADS = 8
HEAD_DIM = 128
MOBA_BLOCK = 256
MOBA_TOPK = 3
ROPE_DIM = 64
KV_LORA = 256
Q_LORA = 512
ROPE_THETA = 10000.0
NORM_EPS = 1e-6
LANES = 128
MLA_QK_PAD = 2 * LANES
SUM_ROWS = 16
MIXER_ROW_GROUPS = 2
QK_BLOCKS_PER_DOT = 2
ATTN_HEADS_PER_STEP = 4
WEIGHTS_LOOKAHEAD_MOBA = 2
WEIGHTS_LOOKAHEAD_MLA = 3
ATTN_LOOKAHEAD = 3
DENOM_LOG2_RANGE = 60.0
V7X_VMEM_BYTES = 64 * 1024 * 1024
LOG2_E = 1.4426950408889634

F32 = jnp.float32
BF16 = jnp.bfloat16


def _vmem_limit(estimate_bytes):
    return int(min(estimate_bytes * 3 // 2, V7X_VMEM_BYTES * 15 // 16))


def _rms(x, g):
    ms = jnp.mean(x * x, axis=-1, keepdims=True)
    return x * lax.rsqrt(ms + NORM_EPS) * g


def _resident(shape, index_map):
    return pl.BlockSpec(shape, index_map, pipeline_mode=pl.Buffered(1))


def _swiglu_up(h_of, w1_ref, w3_ref, s_ref, rows, ff_chunks):
    for start, size in ff_chunks:
        a = jnp.dot(h_of(), w1_ref[:, start:start + size], preferred_element_type=F32)
        b = jnp.dot(h_of(), w3_ref[:, start:start + size], preferred_element_type=F32)
        silu = a * (1.0 / (1.0 + jnp.exp(-a)))
        s_ref[rows, start:start + size] = (silu * b).astype(BF16)


def _ffn_plain_body(x_ref, g_ref, w1_ref, w3_ref, w2_ref, o_ref, s_ref, *, ff_chunks):
    x = x_ref[...]
    h = _rms(x, g_ref[0:1, :]).astype(BF16)
    _swiglu_up(lambda: h, w1_ref, w3_ref, s_ref, slice(None), ff_chunks)
    y = jnp.dot(s_ref[...], w2_ref[...], preferred_element_type=F32)
    o_ref[...] = x + 0.5 * _rms(y, g_ref[1:2, :])


def _ffn_mixer_body(x_ref, a_ref, wo_ref, g_ref, w1_ref, w3_ref, w2_ref, o_ref, s_ref, *, ff_chunks):
    tm = x_ref.shape[0]
    groups = [slice(r, r + tm // MIXER_ROW_GROUPS) for r in range(0, tm, tm // MIXER_ROW_GROUPS)]
    xs, hs = [], []
    for rows in groups:
        mix = lax.dot_general(a_ref[:, rows], wo_ref[...], (((0,), (0,)), ((), ())),
                              preferred_element_type=F32)
        x = x_ref[rows, :] + _rms(mix, g_ref[0:1, :])
        xs.append(x)
        hs.append(_rms(x, g_ref[1:2, :]).astype(BF16))
    for rows, h in zip(groups, hs):
        _swiglu_up(lambda h=h: h, w1_ref, w3_ref, s_ref, rows, ff_chunks)
    for rows, x in zip(groups, xs):
        y = jnp.dot(s_ref[rows, :], w2_ref[...], preferred_element_type=F32)
        o_ref[rows, :] = x + 0.5 * _rms(y, g_ref[2:3, :])


def _ffn(x2d, gains, w1, w3, w2, layer, half, mixer=None, *, tm=1024):
    mt, d = x2d.shape
    ff = w1.shape[-1]
    chunk = 4 * LANES
    ff_chunks = tuple((s, min(chunk, ff - s)) for s in range(0, ff, chunk))
    est = (4 * tm * d * 4 + 3 * d * ff * 2 + tm * ff * 2 + 4 * tm * chunk * 4 + 2 * tm * d * 4)
    row = pl.BlockSpec((tm, d), lambda i: (i, 0))
    gain_spec = _resident((None, None, 3, d), lambda i: (layer, half, 0, 0))
    weight_specs = [
        _resident((None, None, d, ff), lambda i: (layer, half, 0, 0)),
        _resident((None, None, d, ff), lambda i: (layer, half, 0, 0)),
        _resident((None, None, ff, d), lambda i: (layer, half, 0, 0)),
    ]
    gains3 = gains.reshape(gains.shape[0], 2, 3, d)
    if mixer is None:
        body = _ffn_plain_body
        in_specs = [row, gain_spec] + weight_specs
        operands = (x2d, gains3, w1, w3, w2)
    else:
        body = _ffn_mixer_body
        attn_out, w_o, w_o_layer = mixer
        est += 2 * tm * d * 2 + d * d * 2 + tm * d * 4
        tiles = attn_out.shape[2] // tm
        attn_spec = pl.BlockSpec((None, d, tm), lambda i: (i // tiles, 0, i % tiles))
        in_specs = ([row, attn_spec, _resident((None, d, d), lambda i: (w_o_layer, 0, 0)), gain_spec]
                    + weight_specs)
        operands = (x2d, attn_out, w_o, gains3, w1, w3, w2)
    return pl.pallas_call(
        functools.partial(body, ff_chunks=ff_chunks),
        out_shape=jax.ShapeDtypeStruct((mt, d), F32),
        grid=(mt // tm,),
        in_specs=in_specs,
        out_specs=row,
        scratch_shapes=[pltpu.VMEM((tm, ff), BF16)],
        compiler_params=pltpu.CompilerParams(
            dimension_semantics=("parallel",), vmem_limit_bytes=_vmem_limit(est)),
        name=f"ffn_l{layer}_h{half}",
    )(*operands)


def _rope_group(x, cos_t, sin_t):
    return x * cos_t + pltpu.roll(x, LANES // 2, 1) * sin_t


def _moba_qkv_kernel(x_ref, g_ref, w_ref, cos_ref, sin_ref, q_ref, k_ref, vt_ref):
    d = D_MODEL
    h = _rms(x_ref[0], g_ref[...]).astype(BF16)
    cos_t = cos_ref[...]
    sin_t = sin_ref[...]
    q = jnp.dot(h, w_ref[:, 0:d], preferred_element_type=F32)
    k = jnp.dot(h, w_ref[:, d:2 * d], preferred_element_type=F32)
    v = jnp.dot(h, w_ref[:, 2 * d:3 * d], preferred_element_type=F32)
    for hd in range(N_HEADS):
        sl = slice(hd * HEAD_DIM, (hd + 1) * HEAD_DIM)
        q_ref[0, :, sl] = _rope_group(q[:, sl], cos_t, sin_t).astype(BF16)
        k_ref[0, :, sl] = _rope_group(k[:, sl], cos_t, sin_t).astype(BF16)
    vt_ref[0] = v.T.astype(BF16)


def _moba_qkv(x3d, gain, w_qkv, layer, cos_t, sin_t, *, tm=1024):
    b, s, d = x3d.shape
    est = 2 * tm * d * 4 + d * 3 * d * 2 + 4 * tm * LANES * 4 + 6 * tm * d * 2 + 4 * tm * d * 4
    return pl.pallas_call(
        _moba_qkv_kernel,
        out_shape=(jax.ShapeDtypeStruct((b, s, d), BF16),
                   jax.ShapeDtypeStruct((b, s, d), BF16),
                   jax.ShapeDtypeStruct((b, d, s), BF16)),
        grid=(b, s // tm),
        in_specs=[
            pl.BlockSpec((1, tm, d), lambda i, j: (i, j, 0)),
            _resident((1, d), lambda i, j: (0, 0)),
            _resident((None, d, 3 * d), lambda i, j: (layer, 0, 0)),
            pl.BlockSpec((tm, LANES), lambda i, j: (j, 0)),
            pl.BlockSpec((tm, LANES), lambda i, j: (j, 0)),
        ],
        out_specs=(pl.BlockSpec((1, tm, d), lambda i, j: (i, j, 0)),
                   pl.BlockSpec((1, tm, d), lambda i, j: (i, j, 0)),
                   pl.BlockSpec((1, d, tm), lambda i, j: (i, 0, j))),
        compiler_params=pltpu.CompilerParams(
            dimension_semantics=("parallel", "parallel"), vmem_limit_bytes=_vmem_limit(est)),
        name="moba_qkv",
    )(x3d, gain.reshape(1, d), w_qkv, cos_t, sin_t)


def _mla_kv_kernel(x_ref, gin_ref, wd_ref, gkv_ref, wu_ref, cos_ref, sin_ref, ka_ref, vt_ref):
    h = _rms(x_ref[0], gin_ref[...]).astype(BF16)
    ckr = jnp.dot(h, wd_ref[...], preferred_element_type=F32)
    c_kv = _rms(ckr[:, :KV_LORA], gkv_ref[...]).astype(BF16)
    k_rope = _rope_group(ckr[:, KV_LORA:], cos_ref[...], sin_ref[...]).astype(BF16)
    kv = jnp.dot(c_kv, wu_ref[...], preferred_element_type=F32)
    for hd in range(N_HEADS):
        base = hd * MLA_QK_PAD
        ka_ref[0, :, base:base + HEAD_DIM] = kv[:, base:base + HEAD_DIM].astype(BF16)
        ka_ref[0, :, base + HEAD_DIM:base + MLA_QK_PAD] = k_rope
        vt_ref[0, hd * HEAD_DIM:(hd + 1) * HEAD_DIM, :] = (
            kv[:, base + HEAD_DIM:base + MLA_QK_PAD].T.astype(BF16))


def _mla_kv(x3d, kv_in_norm, wd_pad, kv_norm, w_ukv, cos_t, sin_t, *, tm=1024):
    b, s, d = x3d.shape
    nd = wd_pad.shape[1]
    nu = w_ukv.shape[1]
    est = 2 * tm * d * 4 + d * nd * 2 + KV_LORA * nu * 2 + 2 * tm * nu * 2 + 2 * tm * d * 2 + 3 * tm * nu * 4
    return pl.pallas_call(
        _mla_kv_kernel,
        out_shape=(jax.ShapeDtypeStruct((b, s, N_HEADS * MLA_QK_PAD), BF16),
                   jax.ShapeDtypeStruct((b, d, s), BF16)),
        grid=(b, s // tm),
        in_specs=[
            pl.BlockSpec((1, tm, d), lambda i, j: (i, j, 0)),
            _resident((1, d), lambda i, j: (0, 0)),
            _resident((d, nd), lambda i, j: (0, 0)),
            _resident((1, KV_LORA), lambda i, j: (0, 0)),
            _resident((KV_LORA, nu), lambda i, j: (0, 0)),
            pl.BlockSpec((tm, LANES), lambda i, j: (j, 0)),
            pl.BlockSpec((tm, LANES), lambda i, j: (j, 0)),
        ],
        out_specs=(pl.BlockSpec((1, tm, N_HEADS * MLA_QK_PAD), lambda i, j: (i, j, 0)),
                   pl.BlockSpec((1, d, tm), lambda i, j: (i, 0, j))),
        compiler_params=pltpu.CompilerParams(
            dimension_semantics=("parallel", "parallel"), vmem_limit_bytes=_vmem_limit(est)),
        name="mla_shared_kv",
    )(x3d, kv_in_norm.reshape(1, d), wd_pad, kv_norm.reshape(1, KV_LORA), w_ukv, cos_t, sin_t)


def _mla_q_kernel(x_ref, g_ref, wd_ref, gq_ref, wu_ref, cos_ref, sin_ref, qa_ref):
    h = _rms(x_ref[0], g_ref[...]).astype(BF16)
    cq = jnp.dot(h, wd_ref[...], preferred_element_type=F32)
    cqn = _rms(cq, gq_ref[...]).astype(BF16)
    qa = jnp.dot(cqn, wu_ref[...], preferred_element_type=F32)
    cos_t = cos_ref[...]
    sin_t = sin_ref[...]
    for hd in range(N_HEADS):
        base = hd * MLA_QK_PAD
        qa_ref[0, :, base:base + HEAD_DIM] = qa[:, base:base + HEAD_DIM].astype(BF16)
        qa_ref[0, :, base + HEAD_DIM:base + MLA_QK_PAD] = _rope_group(
            qa[:, base + HEAD_DIM:base + MLA_QK_PAD], cos_t, sin_t).astype(BF16)


def _mla_q(x3d, gain, w_dq, q_norm, wu_pad, layer, cos_t, sin_t, *, tm=1024):
    b, s, d = x3d.shape
    nu = wu_pad.shape[-1]
    est = 2 * tm * d * 4 + d * Q_LORA * 2 + Q_LORA * nu * 2 + 2 * tm * nu * 2 + 3 * tm * nu * 4
    return pl.pallas_call(
        _mla_q_kernel,
        out_shape=jax.ShapeDtypeStruct((b, s, nu), BF16),
        grid=(b, s // tm),
        in_specs=[
            pl.BlockSpec((1, tm, d), lambda i, j: (i, j, 0)),
            _resident((1, d), lambda i, j: (0, 0)),
            _resident((None, d, Q_LORA), lambda i, j: (layer, 0, 0)),
            _resident((1, Q_LORA), lambda i, j: (0, 0)),
            _resident((None, Q_LORA, nu), lambda i, j: (layer, 0, 0)),
            pl.BlockSpec((tm, LANES), lambda i, j: (j, 0)),
            pl.BlockSpec((tm, LANES), lambda i, j: (j, 0)),
        ],
        out_specs=pl.BlockSpec((1, tm, nu), lambda i, j: (i, j, 0)),
        compiler_params=pltpu.CompilerParams(
            dimension_semantics=("parallel", "parallel"), vmem_limit_bytes=_vmem_limit(est)),
        name="mla_q",
    )(x3d, gain.reshape(1, d), w_dq, q_norm.reshape(1, Q_LORA), wu_pad, cos_t, sin_t)


def _moba_keep(gate, own):
    rows = [gate[n:n + 1, :] for n in range(own)]
    keep = []
    for n in range(own):
        rank = jnp.zeros_like(rows[n])
        for m in range(own):
            if m < n:
                rank = rank + jnp.where(rows[m] >= rows[n], 1.0, 0.0)
            elif m > n:
                rank = rank + jnp.where(rows[m] > rows[n], 1.0, 0.0)
        keep.append(rank < float(MOBA_TOPK))
    return keep


_NT = (((1,), (1,)), ((), ()))


class _Head:
    def __init__(self, q_ref, k_ref, vt_ref, o_ref, idx, dk, dv, moba):
        self.q_ref, self.k_ref, self.vt_ref, self.o_ref = q_ref, k_ref, vt_ref, o_ref
        self.cols = slice(idx * dk, (idx + 1) * dk)
        self.rows = slice(idx * dv, (idx + 1) * dv)
        self.dv, self.moba = dv, moba
        self.blk = MOBA_BLOCK
        self.nb = q_ref.shape[1] // self.blk
        self.ones_rows = jnp.ones((SUM_ROWS, q_ref.shape[1]), BF16)
        if moba:
            sub_i = lax.broadcasted_iota(jnp.int32, (8, dk), 0)
            km = jnp.zeros((8, dk), F32)
            for n in range(self.nb - 1):
                row = jnp.sum(self.k(n).astype(F32), axis=0, keepdims=True)
                km = jnp.where(sub_i == n, row * (1.0 / self.blk), km)
            km_hi = km.astype(BF16)
            km_lo = (km - km_hi.astype(F32)).astype(BF16)
            self.km_hl = jnp.concatenate([km_hi, km_lo], axis=0)

    def q(self, qi):
        return self.q_ref[0, qi * self.blk:(qi + 1) * self.blk, self.cols]

    def k(self, n):
        return self.k_ref[0, n * self.blk:(n + 1) * self.blk, self.cols]

    def k_span(self, n0, count):
        return self.k_ref[0, n0 * self.blk:(n0 + count) * self.blk, self.cols]

    def keep_rows(self, qi, q_blk):
        if not (self.moba and qi > MOBA_TOPK):
            return None
        g2 = lax.dot_general(self.km_hl, q_blk, _NT, preferred_element_type=F32)
        return _moba_keep(g2[0:8, :] + g2[8:16, :], qi)

    def values(self, qi, p_list):
        p_t = p_list[0] if len(p_list) == 1 else jnp.concatenate(p_list, axis=0)
        kv = (qi + 1) * self.blk
        vt_aug = jnp.concatenate([self.vt_ref[0, self.rows, 0:kv], self.ones_rows[:, 0:kv]], axis=0)
        out_aug = jnp.dot(vt_aug, p_t, preferred_element_type=F32)
        denom = out_aug[self.dv:self.dv + 1, :]
        out_t = out_aug[0:self.dv, :] * (1.0 / denom)
        self.o_ref[0, self.rows, qi * self.blk:(qi + 1) * self.blk] = out_t.astype(BF16)
        return denom


def _attn_kernel(q_ref, k_ref, vt_ref, o_ref, *, moba, exp_scale, heads):
    blk = MOBA_BLOCK
    dk = q_ref.shape[2] // heads
    dv = vt_ref.shape[1] // heads
    key_i = lax.broadcasted_iota(jnp.int32, (blk, blk), 0)
    qry_i = lax.broadcasted_iota(jnp.int32, (blk, blk), 1)
    causal_bias = jnp.where(key_i <= qry_i, 0.0, -jnp.inf)
    head_views = [_Head(q_ref, k_ref, vt_ref, o_ref, i, dk, dv, moba) for i in range(heads)]

    lo, hi = 2.0 ** -DENOM_LOG2_RANGE, 2.0 ** DENOM_LOG2_RANGE
    def weights(hv, qi):
        q_blk = hv.q(qi)
        keep = hv.keep_rows(qi, q_blk)
        p_list = []
        t_blocks = []
        for n0 in range(0, qi + 1, QK_BLOCKS_PER_DOT):
            cnt = min(QK_BLOCKS_PER_DOT, qi + 1 - n0)
            t_grp = lax.dot_general(hv.k_span(n0, cnt), q_blk, _NT,
                                    preferred_element_type=F32) * exp_scale
            t_blocks += [t_grp[i * blk:(i + 1) * blk, :] for i in range(cnt)]
        for n in range(qi + 1):
            t = t_blocks[n]
            if n == qi:
                t = t + causal_bias
            p = jnp.exp2(t).astype(BF16)
            if n < qi and keep is not None:
                p = p * jnp.where(keep[n], 1.0, 0.0).astype(BF16)
            p_list.append(p)
        return p_list

    items = [(hv, qi) for hv in head_views for qi in reversed(range(hv.nb))]
    suspect = None
    lookahead = WEIGHTS_LOOKAHEAD_MOBA if moba else WEIGHTS_LOOKAHEAD_MLA
    queue = [weights(*it) for it in items[:lookahead]]
    for idx, (hv, qi) in enumerate(items):
        if idx + lookahead < len(items):
            queue.append(weights(*items[idx + lookahead]))
        denom = hv.values(qi, queue.pop(0))
        flag = jnp.where(jnp.logical_and(denom >= lo, denom <= hi), 0.0, 1.0)
        suspect = flag if suspect is None else jnp.maximum(suspect, flag)

    @pl.when(jnp.max(suspect) > 0.0)
    def _():
        for hv in head_views:
            _shifted_softmax_pass(hv, causal_bias, exp_scale)


def _shifted_softmax_pass(hv, causal_bias, exp_scale):
    nb, blk = hv.nb, hv.blk

    def logits_begin(qi):
        q_blk = hv.q(qi)
        keep = hv.keep_rows(qi, q_blk)
        sel_bias = None if keep is None else [jnp.where(kp, 0.0, -jnp.inf) for kp in keep]
        return dict(qi=qi, q=q_blk, bias=sel_bias, s=[], m8=None)

    def logits_step(st, n):
        qi = st["qi"]
        s = lax.dot_general(hv.k(n), st["q"], _NT, preferred_element_type=F32)
        if n == qi:
            s = s + causal_bias
        st["s"].append(s)
        bm8 = jnp.max(s.reshape(blk // 8, 8, blk), axis=0)
        if n < qi and st["bias"] is not None:
            bm8 = bm8 + st["bias"][n]
        st["m8"] = bm8 if st["m8"] is None else jnp.maximum(st["m8"], bm8)

    def weights_step(st, m, n):
        m_n = m - st["bias"][n] if (n < st["qi"] and st["bias"] is not None) else m
        return jnp.exp2((st["s"][n] - m_n) * exp_scale).astype(BF16)

    def logits_all(qi):
        st = logits_begin(qi)
        for n in range(qi + 1):
            logits_step(st, n)
        return st

    ahead = [logits_all(qi) for qi in range(min(ATTN_LOOKAHEAD, nb))]
    for qi in range(nb):
        if qi + ATTN_LOOKAHEAD < nb:
            ahead.append(logits_all(qi + ATTN_LOOKAHEAD))
        cur = ahead.pop(0)
        m = jnp.max(cur["m8"], axis=0, keepdims=True)
        hv.values(qi, [weights_step(cur, m, n) for n in range(qi + 1)])


def _attention(q, k, vt, *, moba, qk_dim, scale, heads=ATTN_HEADS_PER_STEP):
    b, s, _ = q.shape
    dv = HEAD_DIM
    est = heads * (2 * (2 * s * qk_dim * 2 + 2 * s * dv * 2) + 24 * MOBA_BLOCK * s * 4)
    return pl.pallas_call(
        functools.partial(_attn_kernel, moba=moba, exp_scale=scale * LOG2_E, heads=heads),
        out_shape=jax.ShapeDtypeStruct((b, N_HEADS * dv, s), BF16),
        grid=(b, N_HEADS // heads),
        in_specs=[
            pl.BlockSpec((1, s, heads * qk_dim), lambda i, j: (i, 0, j)),
            pl.BlockSpec((1, s, heads * qk_dim), lambda i, j: (i, 0, j)),
            pl.BlockSpec((1, heads * dv, s), lambda i, j: (i, j, 0)),
        ],
        out_specs=pl.BlockSpec((1, heads * dv, s), lambda i, j: (i, j, 0)),
        compiler_params=pltpu.CompilerParams(
            dimension_semantics=("parallel", "parallel"), vmem_limit_bytes=_vmem_limit(est)),
        name="moba_attn" if moba else "mla_attn",
    )(q, k, vt)


def _rope_tables(seq_len, dim):
    inv_freq = 1.0 / (ROPE_THETA ** (jnp.arange(0, dim, 2, dtype=F32) / dim))
    ang = jnp.arange(seq_len, dtype=F32)[:, None] * inv_freq[None, :]
    return jnp.cos(ang), jnp.sin(ang)


def _rope_group_layout(main, rope):
    half = ROPE_DIM // 2
    zeros = jnp.zeros(rope.shape[:-1] + (LANES // 2 - half,), rope.dtype)
    return jnp.concatenate([main, rope[..., :half], zeros, rope[..., half:], zeros], axis=-1)


def kernel(x, norm_gains, ffn_w1, ffn_w3, ffn_w2, moba_w_qkv, moba_w_o, mla_w_dq, mla_q_norm,
           mla_w_uq, mla_w_o, kv_in_norm, w_dkv_kr, kv_norm, w_ukv):
    b, s, d = x.shape
    depth = norm_gains.shape[0]
    n_a = moba_w_qkv.shape[0]

    cos_a, sin_a = _rope_tables(s, HEAD_DIM)
    cos_moba = jnp.concatenate([cos_a, cos_a], axis=1)
    sin_moba = jnp.concatenate([-sin_a, sin_a], axis=1)
    cos_r, sin_r = _rope_tables(s, ROPE_DIM)
    zpad = jnp.zeros_like(cos_r)
    cos_mla = jnp.concatenate([cos_r, zpad, cos_r, zpad], axis=1)
    sin_mla = jnp.concatenate([-sin_r, zpad, sin_r, zpad], axis=1)

    w1 = ffn_w1.astype(BF16)
    w3 = ffn_w3.astype(BF16)
    w2 = ffn_w2.astype(BF16)
    w_qkv = moba_w_qkv.astype(BF16)
    w_o_a = moba_w_o.astype(BF16)
    w_o_b = mla_w_o.astype(BF16)
    w_dq = mla_w_dq.astype(BF16)
    wd_pad = _rope_group_layout(w_dkv_kr[:, :KV_LORA], w_dkv_kr[:, KV_LORA:]).astype(BF16)
    uq = mla_w_uq.reshape(mla_w_uq.shape[0], Q_LORA, N_HEADS, HEAD_DIM + ROPE_DIM)
    wu_pad = _rope_group_layout(uq[..., :HEAD_DIM], uq[..., HEAD_DIM:])
    wu_pad = wu_pad.reshape(mla_w_uq.shape[0], Q_LORA, N_HEADS * MLA_QK_PAD).astype(BF16)
    w_ukv_b = w_ukv.astype(BF16)

    x2 = x.reshape(b * s, d)
    k_aug = vt_shared = None
    for l in range(depth):
        g = norm_gains[l]
        x2 = _ffn(x2, norm_gains, w1, w3, w2, l, 0)
        x3 = x2.reshape(b, s, d)
        if l < n_a:
            q, k, vt = _moba_qkv(x3, g[2], w_qkv, l, cos_moba, sin_moba)
            o = _attention(q, k, vt, moba=True, qk_dim=HEAD_DIM, scale=HEAD_DIM ** -0.5)
            mixer = (o, w_o_a, l)
        else:
            j = l - n_a
            q_aug = _mla_q(x3, g[2], w_dq, mla_q_norm[j], wu_pad, j, cos_mla, sin_mla)
            o = _attention(q_aug, k_aug, vt_shared, moba=False, qk_dim=MLA_QK_PAD,
                           scale=(HEAD_DIM + ROPE_DIM) ** -0.5)
            mixer = (o, w_o_b, j)
        x2 = _ffn(x2, norm_gains, w1, w3, w2, l, 1, mixer)
        if l == n_a - 1:
            k_aug, vt_shared = _mla_kv(x2.reshape(b, s, d), kv_in_norm, wd_pad, kv_norm,
                                       w_ukv_b, cos_mla, sin_mla)
    return x2.reshape(b, s, d)
```

```python
import functools

import jax
import jax.numpy as jnp
from jax import lax
from jax.experimental import pallas as pl
from jax.experimental.pallas import tpu as pltpu

D_MODEL = 1024
N_HEADS = 8
HEAD_DIM = 128
MOBA_BLOCK = 256
MOBA_TOPK = 3
ROPE_DIM = 64
KV_LORA = 256
Q_LORA = 512
ROPE_THETA = 10000.0
NORM_EPS = 1e-6
LANES = 128
MLA_QK_PAD = 2 * LANES
SUM_ROWS = 16
MIXER_ROW_GROUPS = 2
QK_BLOCKS_PER_DOT = 2
ATTN_HEADS_PER_STEP = 4
WEIGHTS_LOOKAHEAD_MOBA = 2
WEIGHTS_LOOKAHEAD_MLA = 3
ATTN_LOOKAHEAD = 3
DENOM_LOG2_RANGE = 60.0
V7X_VMEM_BYTES = 64 * 1024 * 1024
LOG2_E = 1.4426950408889634

F32 = jnp.float32
BF16 = jnp.bfloat16


def _vmem_limit(estimate_bytes):
    return int(min(estimate_bytes * 3 // 2, V7X_VMEM_BYTES * 15 // 16))


def _rms(x, g):
    ms = jnp.mean(x * x, axis=-1, keepdims=True)
    return x * lax.rsqrt(ms + NORM_EPS) * g


def _resident(shape, index_map):
    return pl.BlockSpec(shape, index_map, pipeline_mode=pl.Buffered(1))


def _swiglu_up(h_of, w1_ref, w3_ref, s_ref, rows, ff_chunks):
    for start, size in ff_chunks:
        a = jnp.dot(h_of(), w1_ref[:, start:start + size], preferred_element_type=F32)
        b = jnp.dot(h_of(), w3_ref[:, start:start + size], preferred_element_type=F32)
        silu = a * (1.0 / (1.0 + jnp.exp(-a)))
        s_ref[rows, start:start + size] = (silu * b).astype(BF16)


def _ffn_plain_body(x_ref, g_ref, w1_ref, w3_ref, w2_ref, o_ref, s_ref, *, ff_chunks):
    x = x_ref[...]
    h = _rms(x, g_ref[0:1, :]).astype(BF16)
    _swiglu_up(lambda: h, w1_ref, w3_ref, s_ref, slice(None), ff_chunks)
    y = jnp.dot(s_ref[...], w2_ref[...], preferred_element_type=F32)
    o_ref[...] = x + 0.5 * _rms(y, g_ref[1:2, :])


def _ffn_mixer_body(x_ref, a_ref, wo_ref, g_ref, w1_ref, w3_ref, w2_ref, o_ref, s_ref, *, ff_chunks):
    tm = x_ref.shape[0]
    groups = [slice(r, r + tm // MIXER_ROW_GROUPS) for r in range(0, tm, tm // MIXER_ROW_GROUPS)]
    xs, hs = [], []
    for rows in groups:
        mix = lax.dot_general(a_ref[:, rows], wo_ref[...], (((0,), (0,)), ((), ())),
                              preferred_element_type=F32)
        x = x_ref[rows, :] + _rms(mix, g_ref[0:1, :])
        xs.append(x)
        hs.append(_rms(x, g_ref[1:2, :]).astype(BF16))
    for rows, h in zip(groups, hs):
        _swiglu_up(lambda h=h: h, w1_ref, w3_ref, s_ref, rows, ff_chunks)
    for rows, x in zip(groups, xs):
        y = jnp.dot(s_ref[rows, :], w2_ref[...], preferred_element_type=F32)
        o_ref[rows, :] = x + 0.5 * _rms(y, g_ref[2:3, :])


def _ffn(x2d, gains, w1, w3, w2, layer, half, mixer=None, *, tm=1024):
    mt, d = x2d.shape
    ff = w1.shape[-1]
    chunk = 4 * LANES
    ff_chunks = tuple((s, min(chunk, ff - s)) for s in range(0, ff, chunk))
    est = (4 * tm * d * 4 + 3 * d * ff * 2 + tm * ff * 2 + 4 * tm * chunk * 4 + 2 * tm * d * 4)
    row = pl.BlockSpec((tm, d), lambda i: (i, 0))
    gain_spec = _resident((None, None, 3, d), lambda i: (layer, half, 0, 0))
    weight_specs = [
        _resident((None, None, d, ff), lambda i: (layer, half, 0, 0)),
        _resident((None, None, d, ff), lambda i: (layer, half, 0, 0)),
        _resident((None, None, ff, d), lambda i: (layer, half, 0, 0)),
    ]
    gains3 = gains.reshape(gains.shape[0], 2, 3, d)
    if mixer is None:
        body = _ffn_plain_body
        in_specs = [row, gain_spec] + weight_specs
        operands = (x2d, gains3, w1, w3, w2)
    else:
        body = _ffn_mixer_body
        attn_out, w_o, w_o_layer = mixer
        est += 2 * tm * d * 2 + d * d * 2 + tm * d * 4
        tiles = attn_out.shape[2] // tm
        attn_spec = pl.BlockSpec((None, d, tm), lambda i: (i // tiles, 0, i % tiles))
        in_specs = ([row, attn_spec, _resident((None, d, d), lambda i: (w_o_layer, 0, 0)), gain_spec]
                    + weight_specs)
        operands = (x2d, attn_out, w_o, gains3, w1, w3, w2)
    return pl.pallas_call(
        functools.partial(body, ff_chunks=ff_chunks),
        out_shape=jax.ShapeDtypeStruct((mt, d), F32),
        grid=(mt // tm,),
        in_specs=in_specs,
        out_specs=row,
        scratch_shapes=[pltpu.VMEM((tm, ff), BF16)],
        compiler_params=pltpu.CompilerParams(
            dimension_semantics=("parallel",), vmem_limit_bytes=_vmem_limit(est)),
        name=f"ffn_l{layer}_h{half}",
    )(*operands)


def _rope_group(x, cos_t, sin_t):
    return x * cos_t + pltpu.roll(x, LANES // 2, 1) * sin_t


def _moba_qkv_kernel(x_ref, g_ref, w_ref, cos_ref, sin_ref, q_ref, k_ref, vt_ref):
    d = D_MODEL
    h = _rms(x_ref[0], g_ref[...]).astype(BF16)
    cos_t = cos_ref[...]
    sin_t = sin_ref[...]
    q = jnp.dot(h, w_ref[:, 0:d], preferred_element_type=F32)
    k = jnp.dot(h, w_ref[:, d:2 * d], preferred_element_type=F32)
    v = jnp.dot(h, w_ref[:, 2 * d:3 * d], preferred_element_type=F32)
    for hd in range(N_HEADS):
        sl = slice(hd * HEAD_DIM, (hd + 1) * HEAD_DIM)
        q_ref[0, :, sl] = _rope_group(q[:, sl], cos_t, sin_t).astype(BF16)
        k_ref[0, :, sl] = _rope_group(k[:, sl], cos_t, sin_t).astype(BF16)
    vt_ref[0] = v.T.astype(BF16)


def _moba_qkv(x3d, gain, w_qkv, layer, cos_t, sin_t, *, tm=1024):
    b, s, d = x3d.shape
    est = 2 * tm * d * 4 + d * 3 * d * 2 + 4 * tm * LANES * 4 + 6 * tm * d * 2 + 4 * tm * d * 4
    return pl.pallas_call(
        _moba_qkv_kernel,
        out_shape=(jax.ShapeDtypeStruct((b, s, d), BF16),
                   jax.ShapeDtypeStruct((b, s, d), BF16),
                   jax.ShapeDtypeStruct((b, d, s), BF16)),
        grid=(b, s // tm),
        in_specs=[
            pl.BlockSpec((1, tm, d), lambda i, j: (i, j, 0)),
            _resident((1, d), lambda i, j: (0, 0)),
            _resident((None, d, 3 * d), lambda i, j: (layer, 0, 0)),
            pl.BlockSpec((tm, LANES), lambda i, j: (j, 0)),
            pl.BlockSpec((tm, LANES), lambda i, j: (j, 0)),
        ],
        out_specs=(pl.BlockSpec((1, tm, d), lambda i, j: (i, j, 0)),
                   pl.BlockSpec((1, tm, d), lambda i, j: (i, j, 0)),
                   pl.BlockSpec((1, d, tm), lambda i, j: (i, 0, j))),
        compiler_params=pltpu.CompilerParams(
            dimension_semantics=("parallel", "parallel"), vmem_limit_bytes=_vmem_limit(est)),
        name="moba_qkv",
    )(x3d, gain.reshape(1, d), w_qkv, cos_t, sin_t)


def _mla_kv_kernel(x_ref, gin_ref, wd_ref, gkv_ref, wu_ref, cos_ref, sin_ref, ka_ref, vt_ref):
    h = _rms(x_ref[0], gin_ref[...]).astype(BF16)
    ckr = jnp.dot(h, wd_ref[...], preferred_element_type=F32)
    c_kv = _rms(ckr[:, :KV_LORA], gkv_ref[...]).astype(BF16)
    k_rope = _rope_group(ckr[:, KV_LORA:], cos_ref[...], sin_ref[...]).astype(BF16)
    kv = jnp.dot(c_kv, wu_ref[...], preferred_element_type=F32)
    for hd in range(N_HEADS):
        base = hd * MLA_QK_PAD
        ka_ref[0, :, base:base + HEAD_DIM] = kv[:, base:base + HEAD_DIM].astype(BF16)
        ka_ref[0, :, base + HEAD_DIM:base + MLA_QK_PAD] = k_rope
        vt_ref[0, hd * HEAD_DIM:(hd + 1) * HEAD_DIM, :] = (
            kv[:, base + HEAD_DIM:base + MLA_QK_PAD].T.astype(BF16))


def _mla_kv(x3d, kv_in_norm, wd_pad, kv_norm, w_ukv, cos_t, sin_t, *, tm=1024):
    b, s, d = x3d.shape
    nd = wd_pad.shape[1]
    nu = w_ukv.shape[1]
    est = 2 * tm * d * 4 + d * nd * 2 + KV_LORA * nu * 2 + 2 * tm * nu * 2 + 2 * tm * d * 2 + 3 * tm * nu * 4
    return pl.pallas_call(
        _mla_kv_kernel,
        out_shape=(jax.ShapeDtypeStruct((b, s, N_HEADS * MLA_QK_PAD), BF16),
                   jax.ShapeDtypeStruct((b, d, s), BF16)),
        grid=(b, s // tm),
        in_specs=[
            pl.BlockSpec((1, tm, d), lambda i, j: (i, j, 0)),
            _resident((1, d), lambda i, j: (0, 0)),
            _resident((d, nd), lambda i, j: (0, 0)),
            _resident((1, KV_LORA), lambda i, j: (0, 0)),
            _resident((KV_LORA, nu), lambda i, j: (0, 0)),
            pl.BlockSpec((tm, LANES), lambda i, j: (j, 0)),
            pl.BlockSpec((tm, LANES), lambda i, j: (j, 0)),
        ],
        out_specs=(pl.BlockSpec((1, tm, N_HEADS * MLA_QK_PAD), lambda i, j: (i, j, 0)),
                   pl.BlockSpec((1, d, tm), lambda i, j: (i, 0, j))),
        compiler_params=pltpu.CompilerParams(
            dimension_semantics=("parallel", "parallel"), vmem_limit_bytes=_vmem_limit(est)),
        name="mla_shared_kv",
    )(x3d, kv_in_norm.reshape(1, d), wd_pad, kv_norm.reshape(1, KV_LORA), w_ukv, cos_t, sin_t)


def _mla_q_kernel(x_ref, g_ref, wd_ref, gq_ref, wu_ref, cos_ref, sin_ref, qa_ref):
    h = _rms(x_ref[0], g_ref[...]).astype(BF16)
    cq = jnp.dot(h, wd_ref[...], preferred_element_type=F32)
    cqn = _rms(cq, gq_ref[...]).astype(BF16)
    qa = jnp.dot(cqn, wu_ref[...], preferred_element_type=F32)
    cos_t = cos_ref[...]
    sin_t = sin_ref[...]
    for hd in range(N_HEADS):
        base = hd * MLA_QK_PAD
        qa_ref[0, :, base:base + HEAD_DIM] = qa[:, base:base + HEAD_DIM].astype(BF16)
        qa_ref[0, :, base + HEAD_DIM:base + MLA_QK_PAD] = _rope_group(
            qa[:, base + HEAD_DIM:base + MLA_QK_PAD], cos_t, sin_t).astype(BF16)


def _mla_q(x3d, gain, w_dq, q_norm, wu_pad, layer, cos_t, sin_t, *, tm=1024):
    b, s, d = x3d.shape
    nu = wu_pad.shape[-1]
    est = 2 * tm * d * 4 + d * Q_LORA * 2 + Q_LORA * nu * 2 + 2 * tm * nu * 2 + 3 * tm * nu * 4
    return pl.pallas_call(
        _mla_q_kernel,
        out_shape=jax.ShapeDtypeStruct((b, s, nu), BF16),
        grid=(b, s // tm),
        in_specs=[
            pl.BlockSpec((1, tm, d), lambda i, j: (i, j, 0)),
            _resident((1, d), lambda i, j: (0, 0)),
            _resident((None, d, Q_LORA), lambda i, j: (layer, 0, 0)),
            _resident((1, Q_LORA), lambda i, j: (0, 0)),
            _resident((None, Q_LORA, nu), lambda i, j: (layer, 0, 0)),
            pl.BlockSpec((tm, LANES), lambda i, j: (j, 0)),
            pl.BlockSpec((tm, LANES), lambda i, j: (j, 0)),
        ],
        out_specs=pl.BlockSpec((1, tm, nu), lambda i, j: (i, j, 0)),
        compiler_params=pltpu.CompilerParams(
            dimension_semantics=("parallel", "parallel"), vmem_limit_bytes=_vmem_limit(est)),
        name="mla_q",
    )(x3d, gain.reshape(1, d), w_dq, q_norm.reshape(1, Q_LORA), wu_pad, cos_t, sin_t)


def _moba_keep(gate, own):
    rows = [gate[n:n + 1, :] for n in range(own)]
    keep = []
    for n in range(own):
        rank = jnp.zeros_like(rows[n])
        for m in range(own):
            if m < n:
                rank = rank + jnp.where(rows[m] >= rows[n], 1.0, 0.0)
            elif m > n:
                rank = rank + jnp.where(rows[m] > rows[n], 1.0, 0.0)
        keep.append(rank < float(MOBA_TOPK))
    return keep


_NT = (((1,), (1,)), ((), ()))


class _Head:
    def __init__(self, q_ref, k_ref, vt_ref, o_ref, idx, dk, dv, moba):
        self.q_ref, self.k_ref, self.vt_ref, self.o_ref = q_ref, k_ref, vt_ref, o_ref
        self.cols = slice(idx * dk, (idx + 1) * dk)
        self.rows = slice(idx * dv, (idx + 1) * dv)
        self.dv, self.moba = dv, moba
        self.blk = MOBA_BLOCK
        self.nb = q_ref.shape[1] // self.blk
        self.ones_rows = jnp.ones((SUM_ROWS, q_ref.shape[1]), BF16)
        if moba:
            sub_i = lax.broadcasted_iota(jnp.int32, (8, dk), 0)
            km = jnp.zeros((8, dk), F32)
            for n in range(self.nb - 1):
                row = jnp.sum(self.k(n).astype(F32), axis=0, keepdims=True)
                km = jnp.where(sub_i == n, row * (1.0 / self.blk), km)
            km_hi = km.astype(BF16)
            km_lo = (km - km_hi.astype(F32)).astype(BF16)
            self.km_hl = jnp.concatenate([km_hi, km_lo], axis=0)

    def q(self, qi):
        return self.q_ref[0, qi * self.blk:(qi + 1) * self.blk, self.cols]

    def k(self, n):
        return self.k_ref[0, n * self.blk:(n + 1) * self.blk, self.cols]

    def k_span(self, n0, count):
        return self.k_ref[0, n0 * self.blk:(n0 + count) * self.blk, self.cols]

    def keep_rows(self, qi, q_blk):
        if not (self.moba and qi > MOBA_TOPK):
            return None
        g2 = lax.dot_general(self.km_hl, q_blk, _NT, preferred_element_type=F32)
        return _moba_keep(g2[0:8, :] + g2[8:16, :], qi)

    def values(self, qi, p_list):
        p_t = p_list[0] if len(p_list) == 1 else jnp.concatenate(p_list, axis=0)
        kv = (qi + 1) * self.blk
        vt_aug = jnp.concatenate([self.vt_ref[0, self.rows, 0:kv], self.ones_rows[:, 0:kv]], axis=0)
        out_aug = jnp.dot(vt_aug, p_t, preferred_element_type=F32)
        denom = out_aug[self.dv:self.dv + 1, :]
        out_t = out_aug[0:self.dv, :] * (1.0 / denom)
        self.o_ref[0, self.rows, qi * self.blk:(qi + 1) * self.blk] = out_t.astype(BF16)
        return denom


def _attn_kernel(q_ref, k_ref, vt_ref, o_ref, *, moba, exp_scale, heads):
    blk = MOBA_BLOCK
    dk = q_ref.shape[2] // heads
    dv = vt_ref.shape[1] // heads
    key_i = lax.broadcasted_iota(jnp.int32, (blk, blk), 0)
    qry_i = lax.broadcasted_iota(jnp.int32, (blk, blk), 1)
    causal_bias = jnp.where(key_i <= qry_i, 0.0, -jnp.inf)
    head_views = [_Head(q_ref, k_ref, vt_ref, o_ref, i, dk, dv, moba) for i in range(heads)]

    lo, hi = 2.0 ** -DENOM_LOG2_RANGE, 2.0 ** DENOM_LOG2_RANGE

    def weights(hv, qi):
        q_blk = hv.q(qi)
        keep = hv.keep_rows(qi, q_blk)
        p_list = []
        for n0 in range(0, qi + 1, QK_BLOCKS_PER_DOT):
            cnt = min(QK_BLOCKS_PER_DOT, qi + 1 - n0)
            t_grp = lax.dot_general(hv.k_span(n0, cnt), q_blk, _NT,
                                    preferred_element_type=F32) * exp_scale
            for n in range(n0, n0 + cnt):
                t = t_grp[(n - n0) * blk:(n - n0 + 1) * blk, :]
                if n == qi:
                    t = t + causal_bias
                p = jnp.exp2(t).astype(BF16)
                if n < qi and keep is not None:
                    p = p * jnp.where(keep[n], 1.0, 0.0).astype(BF16)
                p_list.append(p)
        return p_list

    items = [(hv, qi) for hv in head_views for qi in reversed(range(hv.nb))]
    suspect = None
    lookahead = WEIGHTS_LOOKAHEAD_MOBA if moba else WEIGHTS_LOOKAHEAD_MLA
    queue = [weights(*it) for it in items[:lookahead]]
    for idx, (hv, qi) in enumerate(items):
        if idx + lookahead < len(items):
            queue.append(weights(*items[idx + lookahead]))
        denom = hv.values(qi, queue.pop(0))
        flag = jnp.where(jnp.logical_and(denom >= lo, denom <= hi), 0.0, 1.0)
        suspect = flag if suspect is None else jnp.maximum(suspect, flag)

    @pl.when(jnp.max(suspect) > 0.0)
    def _():
        for hv in head_views:
            _shifted_softmax_pass(hv, causal_bias, exp_scale)


def _shifted_softmax_pass(hv, causal_bias, exp_scale):
    nb, blk = hv.nb, hv.blk

    def logits_begin(qi):
        q_blk = hv.q(qi)
        keep = hv.keep_rows(qi, q_blk)
        sel_bias = None if keep is None else [jnp.where(kp, 0.0, -jnp.inf) for kp in keep]
        return dict(qi=qi, q=q_blk, bias=sel_bias, s=[], m8=None)

    def logits_step(st, n):
        qi = st["qi"]
        s = lax.dot_general(hv.k(n), st["q"], _NT, preferred_element_type=F32)
        if n == qi:
            s = s + causal_bias
        st["s"].append(s)
        bm8 = jnp.max(s.reshape(blk // 8, 8, blk), axis=0)
        if n < qi and st["bias"] is not None:
            bm8 = bm8 + st["bias"][n]
        st["m8"] = bm8 if st["m8"] is None else jnp.maximum(st["m8"], bm8)

    def weights_step(st, m, n):
        m_n = m - st["bias"][n] if (n < st["qi"] and st["bias"] is not None) else m
        return jnp.exp2((st["s"][n] - m_n) * exp_scale).astype(BF16)

    def logits_all(qi):
        st = logits_begin(qi)
        for n in range(qi + 1):
            logits_step(st, n)
        return st

    ahead = [logits_all(qi) for qi in range(min(ATTN_LOOKAHEAD, nb))]
    for qi in range(nb):
        if qi + ATTN_LOOKAHEAD < nb:
            ahead.append(logits_all(qi + ATTN_LOOKAHEAD))
        cur = ahead.pop(0)
        m = jnp.max(cur["m8"], axis=0, keepdims=True)
        hv.values(qi, [weights_step(cur, m, n) for n in range(qi + 1)])


def _attention(q, k, vt, *, moba, qk_dim, scale, heads=ATTN_HEADS_PER_STEP):
    b, s, _ = q.shape
    dv = HEAD_DIM
    est = heads * (2 * (2 * s * qk_dim * 2 + 2 * s * dv * 2) + 24 * MOBA_BLOCK * s * 4)
    return pl.pallas_call(
        functools.partial(_attn_kernel, moba=moba, exp_scale=scale * LOG2_E, heads=heads),
        out_shape=jax.ShapeDtypeStruct((b, N_HEADS * dv, s), BF16),
        grid=(b, N_HEADS // heads),
        in_specs=[
            pl.BlockSpec((1, s, heads * qk_dim), lambda i, j: (i, 0, j)),
            pl.BlockSpec((1, s, heads * qk_dim), lambda i, j: (i, 0, j)),
            pl.BlockSpec((1, heads * dv, s), lambda i, j: (i, j, 0)),
        ],
        out_specs=pl.BlockSpec((1, heads * dv, s), lambda i, j: (i, j, 0)),
        compiler_params=pltpu.CompilerParams(
            dimension_semantics=("parallel", "parallel"), vmem_limit_bytes=_vmem_limit(est)),
        name="moba_attn" if moba else "mla_attn",
    )(q, k, vt)


def _rope_tables(seq_len, dim):
    inv_freq = 1.0 / (ROPE_THETA ** (jnp.arange(0, dim, 2, dtype=F32) / dim))
    ang = jnp.arange(seq_len, dtype=F32)[:, None] * inv_freq[None, :]
    return jnp.cos(ang), jnp.sin(ang)


def _rope_group_layout(main, rope):
    half = ROPE_DIM // 2
    zeros = jnp.zeros(rope.shape[:-1] + (LANES // 2 - half,), rope.dtype)
    return jnp.concatenate([main, rope[..., :half], zeros, rope[..., half:], zeros], axis=-1)


def kernel(x, norm_gains, ffn_w1, ffn_w3, ffn_w2, moba_w_qkv, moba_w_o, mla_w_dq, mla_q_norm,
           mla_w_uq, mla_w_o, kv_in_norm, w_dkv_kr, kv_norm, w_ukv):
    b, s, d = x.shape
    depth = norm_gains.shape[0]
    n_a = moba_w_qkv.shape[0]

    cos_a, sin_a = _rope_tables(s, HEAD_DIM)
    cos_moba = jnp.concatenate([cos_a, cos_a], axis=1)
    sin_moba = jnp.concatenate([-sin_a, sin_a], axis=1)
    cos_r, sin_r = _rope_tables(s, ROPE_DIM)
    zpad = jnp.zeros_like(cos_r)
    cos_mla = jnp.concatenate([cos_r, zpad, cos_r, zpad], axis=1)
    sin_mla = jnp.concatenate([-sin_r, zpad, sin_r, zpad], axis=1)

    w1 = ffn_w1.astype(BF16)
    w3 = ffn_w3.astype(BF16)
    w2 = ffn_w2.astype(BF16)
    w_qkv = moba_w_qkv.astype(BF16)
    w_o_a = moba_w_o.astype(BF16)
    w_o_b = mla_w_o.astype(BF16)
    w_dq = mla_w_dq.astype(BF16)
    wd_pad = _rope_group_layout(w_dkv_kr[:, :KV_LORA], w_dkv_kr[:, KV_LORA:]).astype(BF16)
    uq = mla_w_uq.reshape(mla_w_uq.shape[0], Q_LORA, N_HEADS, HEAD_DIM + ROPE_DIM)
    wu_pad = _rope_group_layout(uq[..., :HEAD_DIM], uq[..., HEAD_DIM:])
    wu_pad = wu_pad.reshape(mla_w_uq.shape[0], Q_LORA, N_HEADS * MLA_QK_PAD).astype(BF16)
    w_ukv_b = w_ukv.astype(BF16)

    x2 = x.reshape(b * s, d)
    k_aug = vt_shared = None
    for l in range(depth):
        g = norm_gains[l]
        x2 = _ffn(x2, norm_gains, w1, w3, w2, l, 0)
        x3 = x2.reshape(b, s, d)
        if l < n_a:
            q, k, vt = _moba_qkv(x3, g[2], w_qkv, l, cos_moba, sin_moba)
            o = _attention(q, k, vt, moba=True, qk_dim=HEAD_DIM, scale=HEAD_DIM ** -0.5)
            mixer = (o, w_o_a, l)
        else:
            j = l - n_a
            q_aug = _mla_q(x3, g[2], w_dq, mla_q_norm[j], wu_pad, j, cos_mla, sin_mla)
            o = _attention(q_aug, k_aug, vt_shared, moba=False, qk_dim=MLA_QK_PAD,
                           scale=(HEAD_DIM + ROPE_DIM) ** -0.5)
            mixer = (o, w_o_b, j)
        x2 = _ffn(x2, norm_gains, w1, w3, w2, l, 1, mixer)
        if l == n_a - 1:
            k_aug, vt_shared = _mla_kv(x2.reshape(b, s, d), kv_in_norm, wd_pad, kv_norm,
                                       w_ukv_b, cos_mla, sin_mla)
    return x2.reshape(b, s, d)
```
